```python
import jax
import jax.numpy as jnp
from jax import lax
import numpy as np

D_MODEL = 4096
BATCH = 2
SEQ = 8192
DEPTH = 2

GRID_W = 64
CTX_LEN = 256
Q_BLOCK = 128
ROPE_THETA = 10000.0
NORM_EPS = 1e-6
N_MOD = 9
N_BRANCHES = 3

GQA_HEADS = 16
GQA_KV_HEADS = 4
GQA_GROUP = GQA_HEADS // GQA_KV_HEADS
GQA_HEAD_DIM = 128
GQA_SCALE = GQA_HEAD_DIM ** -0.5

MLA_HEADS = 16
MLA_Q_RANK = 1536
MLA_KV_RANK = 512
MLA_NOPE_DIM = 128
MLA_ROPE_DIM = 64
MLA_V_DIM = 128
MLA_SCALE = (MLA_NOPE_DIM + MLA_ROPE_DIM) ** -0.5

LRU_WIDTH = 2048
LRU_BLOCKS = 16
LRU_BLOCK_W = LRU_WIDTH // LRU_BLOCKS
LRU_C = 8.0
CONV_WIDTH = 4
CONV_PAD_LEFT = 1

D_FF = 6144

IN_TOTAL = (GQA_HEADS + 2 * GQA_KV_HEADS) * GQA_HEAD_DIM + MLA_Q_RANK + MLA_KV_RANK + MLA_ROPE_DIM + 2 * LRU_WIDTH + N_BRANCHES * D_MODEL

kernel_name = 'hybrid_gated_gqa_mla_rglru_dit'


def _in_split_points():
    widths = (GQA_HEADS * GQA_HEAD_DIM, GQA_KV_HEADS * GQA_HEAD_DIM, GQA_KV_HEADS * GQA_HEAD_DIM,
              MLA_Q_RANK, MLA_KV_RANK + MLA_ROPE_DIM, LRU_WIDTH, LRU_WIDTH)
    pts, acc = [], 0
    for w in widths:
        acc += w
        pts.append(acc)
    return pts


def rmsnorm(x, g):
    xf = x.astype(jnp.float32)
    y = xf * lax.rsqrt(jnp.mean(xf * xf, axis=-1, keepdims=True) + NORM_EPS)
    return (y * g.astype(jnp.float32)).astype(x.dtype)


def axial_rope(n_tokens, rot_dim):
    rows = n_tokens // GRID_W
    r_idx, c_idx = jnp.meshgrid(jnp.arange(rows), jnp.arange(GRID_W), indexing='ij')
    r_idx = r_idx.reshape(-1).astype(jnp.float32)
    c_idx = c_idx.reshape(-1).astype(jnp.float32)
    n_pairs = rot_dim // 4
    freqs = jnp.power(ROPE_THETA, -jnp.arange(n_pairs, dtype=jnp.float32) / n_pairs)
    ang = jnp.concatenate([r_idx[:, None] * freqs, c_idx[:, None] * freqs], axis=-1)
    return jnp.cos(ang), jnp.sin(ang)


def apply_rope(x, cos, sin):
    xf = x.astype(jnp.float32)
    x1, x2 = jnp.split(xf, 2, axis=-1)
    c = cos[None, :, None, :]
    s = sin[None, :, None, :]
    return jnp.concatenate([x1 * c - x2 * s, x1 * s + x2 * c], axis=-1).astype(x.dtype)


def block_attention(q, k, v, scale):
    B, T, KH, G, dk = q.shape
    nb = T // Q_BLOCK
    qb = jnp.moveaxis(q.reshape(B, nb, Q_BLOCK, KH, G, dk), 1, 0)

    def one_block(qblk):
        s = jnp.einsum('bqhgd,bkhd->bhgqk', qblk, k, preferred_element_type=jnp.float32) * scale
        p = jax.nn.softmax(s, axis=-1).astype(v.dtype)
        return jnp.einsum('bhgqk,bkhe->bqhge', p, v)

    o = lax.map(one_block, qb)
    return jnp.moveaxis(o, 0, 1).reshape(B, T, KH * G * v.shape[-1])


def gqa_q(q, p, rope):
    B, T, _ = q.shape
    q = rmsnorm(q.reshape(B, T, GQA_HEADS, GQA_HEAD_DIM), p['gqa_q_norm'])
    if rope is not None:
        q = apply_rope(q, *rope)
    return q.reshape(B, T, GQA_KV_HEADS, GQA_GROUP, GQA_HEAD_DIM)


def gqa_kv(k, v, p, rope):
    B, T, _ = k.shape
    k = rmsnorm(k.reshape(B, T, GQA_KV_HEADS, GQA_HEAD_DIM), p['gqa_k_norm'])
    if rope is not None:
        k = apply_rope(k, *rope)
    return k, v.reshape(B, T, GQA_KV_HEADS, GQA_HEAD_DIM)


def mla_q(cq, p, rope):
    B, T, _ = cq.shape
    q = (rmsnorm(cq, p['mla_q_norm']) @ p['w_mla_uq']).reshape(B, T, MLA_HEADS, MLA_NOPE_DIM + MLA_ROPE_DIM)
    q_nope, q_rope = jnp.split(q, [MLA_NOPE_DIM], axis=-1)
    if rope is not None:
        q_rope = apply_rope(q_rope, *rope)
    return jnp.concatenate([q_nope, q_rope], axis=-1)[:, :, :, None, :]


def mla_kv(ckv, p, rope):
    B, T, _ = ckv.shape
    c_kv, k_rope = jnp.split(ckv, [MLA_KV_RANK], axis=-1)
    kv = (rmsnorm(c_kv, p['mla_kv_norm']) @ p['w_mla_ukv']).reshape(B, T, MLA_HEADS, MLA_NOPE_DIM + MLA_V_DIM)
    k_nope, v = jnp.split(kv, [MLA_NOPE_DIM], axis=-1)
    k_rope = k_rope[:, :, None, :]
    if rope is not None:
        k_rope = apply_rope(k_rope, *rope)
    k = jnp.concatenate([k_nope, jnp.broadcast_to(k_rope, (B, T, MLA_HEADS, MLA_ROPE_DIM))], axis=-1)
    return k, v


def short_conv(x, w, b):
    T = x.shape[1]
    xp = jnp.pad(x, ((0, 0), (CONV_PAD_LEFT, CONV_WIDTH - 1 - CONV_PAD_LEFT), (0, 0)))
    y = b + xp[:, 0:T] * w[0]
    for j in range(1, CONV_WIDTH):
        y = y + xp[:, j:j + T] * w[j]
    return y


def rglru_coeffs(xc, wa, ba, wi, bi, lam):
    B, T, _ = xc.shape
    xb = xc.reshape(B, T, LRU_BLOCKS, LRU_BLOCK_W)
    r = jax.nn.sigmoid(jnp.einsum('btnd,nde->btne', xb, wa).reshape(B, T, LRU_WIDTH) + ba)
    i = jax.nn.sigmoid(jnp.einsum('btnd,nde->btne', xb, wi).reshape(B, T, LRU_WIDTH) + bi)
    log_a = -LRU_C * r.astype(jnp.float32) * jax.nn.softplus(-lam.astype(jnp.float32))
    a = jnp.exp(log_a)
    u = jnp.sqrt(-jnp.expm1(2.0 * log_a)) * (i * xc).astype(jnp.float32)
    return a, u


def linear_scan(a, u, h0, reverse):
    def step(h, au):
        h = au[0] * h + au[1]
        return h, h
    h_last, hs = lax.scan(step, h0, (jnp.swapaxes(a, 0, 1), jnp.swapaxes(u, 0, 1)), reverse=reverse)
    return jnp.swapaxes(hs, 0, 1), h_last


def rglru_bidir(x_lat, x_ctx, p):
    xl = short_conv(x_lat, p['lru_conv_w'], p['lru_conv_b'])
    xc = short_conv(x_ctx, p['lru_conv_w'], p['lru_conv_b'])
    h0 = jnp.zeros((x_lat.shape[0], LRU_WIDTH), jnp.float32)

    def one_direction(d, reverse):
        gate_args = (p['lru_wa'][d], p['lru_ba'][d], p['lru_wi'][d], p['lru_bi'][d], p['lru_lambda'][d])
        a_c, u_c = rglru_coeffs(xc, *gate_args)
        hs_c, h_ctx_final = linear_scan(a_c, u_c, h0, reverse)
        a_l, u_l = rglru_coeffs(xl, *gate_args)
        hs_l, _ = linear_scan(a_l, u_l, h_ctx_final, reverse)
        return hs_l, hs_c

    fl, fc = one_direction(0, False)
    bl, bc = one_direction(1, True)
    return (fl + bl).astype(x_lat.dtype), (fc + bc).astype(x_ctx.dtype)


def merge_branches(o_gqa, o_mla, lru_y, lru_gate, gate_logits, p):
    y_gqa = o_gqa @ p['w_o_gqa']
    y_mla = o_mla @ p['w_o_mla']
    y_lru = (lru_y * jax.nn.gelu(lru_gate)) @ p['w_o_lru']
    g = jax.nn.sigmoid(gate_logits).reshape(gate_logits.shape[:-1] + (N_BRANCHES, D_MODEL))
    merged = g[..., 0, :] * y_gqa + g[..., 1, :] * y_mla + g[..., 2, :] * y_lru
    return merged @ p['w_out']


def token_mixer(h_lat, h_ctx, p, rope_gqa, rope_mla, ctx_out):
    pts = _in_split_points()
    gq_l, gk_l, gv_l, cq_l, ckv_l, lx_l, lg_l, gate_l = jnp.split(h_lat @ p['w_in'], pts, axis=-1)
    gq_c, gk_c, gv_c, cq_c, ckv_c, lx_c, lg_c, gate_c = jnp.split(h_ctx @ p['w_in'], pts, axis=-1)
    k_gqa_c, v_gqa_c = gqa_kv(gk_c, gv_c, p, None)
    k_gqa_l, v_gqa_l = gqa_kv(gk_l, gv_l, p, rope_gqa)
    k_mla_c, v_mla_c = mla_kv(ckv_c, p, None)
    k_mla_l, v_mla_l = mla_kv(ckv_l, p, rope_mla)
    lru_l, lru_c = rglru_bidir(lx_l, lx_c, p)
    o_gqa_l = block_attention(gqa_q(gq_l, p, rope_gqa), jnp.concatenate([k_gqa_l, k_gqa_c], axis=1),
                              jnp.concatenate([v_gqa_l, v_gqa_c], axis=1), GQA_SCALE)
    o_mla_l = block_attention(mla_q(cq_l, p, rope_mla), jnp.concatenate([k_mla_l, k_mla_c], axis=1),
                              jnp.concatenate([v_mla_l, v_mla_c], axis=1), MLA_SCALE)
    y_lat = merge_branches(o_gqa_l, o_mla_l, lru_l, lg_l, gate_l, p)
    if not ctx_out:
        return y_lat, None
    o_gqa_c = block_attention(gqa_q(gq_c, p, None), k_gqa_c, v_gqa_c, GQA_SCALE)
    o_mla_c = block_attention(mla_q(cq_c, p, None), k_mla_c, v_mla_c, MLA_SCALE)
    y_ctx = merge_branches(o_gqa_c, o_mla_c, lru_c, lg_c, gate_c, p)
    return y_lat, y_ctx


def swiglu(h, w_in, w_out):
    g, u = jnp.split(h @ w_in, 2, axis=-1)
    return (jax.nn.silu(g) * u) @ w_out


def ffn_half(x, mod, i, g, w_in, w_out):
    h = rmsnorm(x, g) * (1 + mod[i + 1]) + mod[i]
    return x + 0.5 * mod[i + 2] * swiglu(h, w_in, w_out)


def setup_inputs(seed: int = 0) -> dict:
    key = jax.random.key(seed)
    ks = iter(jax.random.split(key, 40))
    f32 = jnp.float32
    L, D = DEPTH, D_MODEL

    def nrm(shape, fan_in, scale=1.0):
        return jax.random.normal(next(ks), shape, f32) * (scale * fan_in ** -0.5)

    def gain(shape):
        return 1.0 + 0.05 * jax.random.normal(next(ks), shape, f32)

    def small(shape):
        return 0.02 * jax.random.normal(next(ks), shape, f32)

    a0 = jax.random.uniform(next(ks), (L, 2, LRU_WIDTH), f32, 0.9, 0.999)
    base = a0 ** (1.0 / LRU_C)
    lam = jnp.log(base) - jnp.log1p(-base)
    return {
        'x': jax.random.normal(next(ks), (BATCH, SEQ, D), f32),
        'c': jax.random.normal(next(ks), (BATCH, D), f32),
        'ctx': jax.random.normal(next(ks), (BATCH, CTX_LEN, D), f32),
        'c_ctx': jax.random.normal(next(ks), (D,), f32),
        'w_mod': nrm((L, D, N_MOD * D), D, 0.5),
        'b_mod': small((L, N_MOD * D)),
        'g_norm': gain((L, 3, D)),
        'w_ff1_in': nrm((L, D, 2 * D_FF), D),
        'w_ff1_out': nrm((L, D_FF, D), D_FF),
        'w_ff2_in': nrm((L, D, 2 * D_FF), D),
        'w_ff2_out': nrm((L, D_FF, D), D_FF),
        'w_in': nrm((L, D, IN_TOTAL), D),
        'gqa_q_norm': gain((L, GQA_HEAD_DIM)),
        'gqa_k_norm': gain((L, GQA_HEAD_DIM)),
        'mla_q_norm': gain((L, MLA_Q_RANK)),
        'mla_kv_norm': gain((L, MLA_KV_RANK)),
        'w_mla_uq': nrm((L, MLA_Q_RANK, MLA_HEADS * (MLA_NOPE_DIM + MLA_ROPE_DIM)), MLA_Q_RANK),
        'w_mla_ukv': nrm((L, MLA_KV_RANK, MLA_HEADS * (MLA_NOPE_DIM + MLA_V_DIM)), MLA_KV_RANK),
        'lru_conv_w': nrm((L, CONV_WIDTH, LRU_WIDTH), CONV_WIDTH),
        'lru_conv_b': small((L, LRU_WIDTH)),
        'lru_wa': nrm((L, 2, LRU_BLOCKS, LRU_BLOCK_W, LRU_BLOCK_W), LRU_BLOCK_W),
        'lru_ba': small((L, 2, LRU_WIDTH)),
        'lru_wi': nrm((L, 2, LRU_BLOCKS, LRU_BLOCK_W, LRU_BLOCK_W), LRU_BLOCK_W),
        'lru_bi': small((L, 2, LRU_WIDTH)),
        'lru_lambda': lam,
        'w_o_gqa': nrm((L, GQA_HEADS * GQA_HEAD_DIM, D), GQA_HEADS * GQA_HEAD_DIM),
        'w_o_mla': nrm((L, MLA_HEADS * MLA_V_DIM, D), MLA_HEADS * MLA_V_DIM),
        'w_o_lru': nrm((L, LRU_WIDTH, D), LRU_WIDTH),
        'w_out': nrm((L, D, D), D),
        'final_norm': gain((D,)),
    }


def reference(x, c, ctx, c_ctx, w_mod, b_mod, g_norm, w_ff1_in, w_ff1_out, w_ff2_in, w_ff2_out, w_in,
              gqa_q_norm, gqa_k_norm, mla_q_norm, mla_kv_norm, w_mla_uq, w_mla_ukv, lru_conv_w, lru_conv_b,
              lru_wa, lru_ba, lru_wi, lru_bi, lru_lambda, w_o_gqa, w_o_mla, w_o_lru, w_out, final_norm):
    n_tok = x.shape[1]
    rope_gqa = axial_rope(n_tok, GQA_HEAD_DIM)
    rope_mla = axial_rope(n_tok, MLA_ROPE_DIM)
    xc = ctx
    for l in range(DEPTH):
        last = l == DEPTH - 1
        p = {
            'w_in': w_in[l], 'gqa_q_norm': gqa_q_norm[l], 'gqa_k_norm': gqa_k_norm[l],
            'mla_q_norm': mla_q_norm[l], 'mla_kv_norm': mla_kv_norm[l],
            'w_mla_uq': w_mla_uq[l], 'w_mla_ukv': w_mla_ukv[l],
            'lru_conv_w': lru_conv_w[l], 'lru_conv_b': lru_conv_b[l],
            'lru_wa': lru_wa[l], 'lru_ba': lru_ba[l], 'lru_wi': lru_wi[l], 'lru_bi': lru_bi[l],
            'lru_lambda': lru_lambda[l],
            'w_o_gqa': w_o_gqa[l], 'w_o_mla': w_o_mla[l], 'w_o_lru': w_o_lru[l], 'w_out': w_out[l],
        }
        mod_l = jnp.split((jax.nn.silu(c) @ w_mod[l] + b_mod[l])[:, None, :], N_MOD, axis=-1)
        mod_c = jnp.split((jax.nn.silu(c_ctx) @ w_mod[l] + b_mod[l])[None, None, :], N_MOD, axis=-1)
        x = ffn_half(x, mod_l, 0, g_norm[l, 0], w_ff1_in[l], w_ff1_out[l])
        xc = ffn_half(xc, mod_c, 0, g_norm[l, 0], w_ff1_in[l], w_ff1_out[l])
        h_l = rmsnorm(x, g_norm[l, 1]) * (1 + mod_l[4]) + mod_l[3]
        h_c = rmsnorm(xc, g_norm[l, 1]) * (1 + mod_c[4]) + mod_c[3]
        y_l, y_c = token_mixer(h_l, h_c, p, rope_gqa, rope_mla, not last)
        x = x + mod_l[5] * y_l
        x = ffn_half(x, mod_l, 6, g_norm[l, 2], w_ff2_in[l], w_ff2_out[l])
        if not last:
            xc = xc + mod_c[5] * y_c
            xc = ffn_half(xc, mod_c, 6, g_norm[l, 2], w_ff2_in[l], w_ff2_out[l])
    return rmsnorm(x, final_norm)
```

```python
import dataclasses
import functools

import jax
import jax.numpy as jnp
from jax import lax
from jax.experimental import pallas as pl
from jax.experimental.pallas import tpu as pltpu

F32 = jnp.float32
BF16 = jnp.bfloat16

LANES = 128
SUBLANES = 8
VMEM_LIMIT = 56 * 1024 * 1024

ROPE_THETA = 10000.0
NORM_EPS = 1e-6
N_MOD = 9
LRU_C = 8.0
CONV_WIDTH = 4
ROW_TILE = 256


@dataclasses.dataclass(frozen=True)
class Dims:
    grid_w: int = 64
    gqa_heads: int = 16
    gqa_kv: int = 4
    mla_heads: int = 16
    q_rank: int = 1536
    kv_rank: int = 512
    rope_dim: int = 64
    lru_w: int = 2048
    d_ff: int = 6144
    proj_pad: int = 1024


DIMS = Dims()


def _params(n_grid):
    return pltpu.CompilerParams(dimension_semantics=("arbitrary",) * n_grid,
                                vmem_limit_bytes=VMEM_LIMIT)


def _tile(dim, pref, align):
    best = None
    t = align
    while t <= min(dim, pref):
        if dim % t == 0:
            best = t
        t += align
    return best if best is not None else dim


@dataclasses.dataclass(frozen=True)
class ProjLayout:
    q: int
    k: int
    v: int
    cq: int
    ckv: int
    krope: int
    lx: int
    lg: int
    gate: int
    total: int


def _proj_layout(dm, d_model):
    q = 0
    k = q + dm.gqa_heads * LANES
    v = k + dm.gqa_kv * LANES
    cq = v + dm.gqa_kv * LANES
    ckv = cq + dm.q_rank
    krope = ckv + dm.kv_rank
    attn_end = krope + LANES
    lx = -(-attn_end // dm.proj_pad) * dm.proj_pad
    lg = lx + dm.lru_w
    gate = lg + dm.lru_w
    total = gate + 3 * d_model
    return ProjLayout(q, k, v, cq, ckv, krope, lx, lg, gate, total)


def _mod_kernel(c_ref, w_ref, b_ref, o_ref):
    c = c_ref[...]
    a = (c * jax.nn.sigmoid(c)).astype(BF16)
    acc = jnp.dot(a, w_ref[...].astype(BF16), preferred_element_type=F32)
    o_ref[...] = acc + b_ref[...]


def _mod_vectors(c_rows, w_mod, b_mod):
    n_layers, d, n = w_mod.shape
    tn = _tile(n, 512, LANES)
    return pl.pallas_call(
        _mod_kernel,
        grid=(n_layers, n // tn),
        in_specs=[
            pl.BlockSpec((SUBLANES, d), lambda l, j: (0, 0)),
            pl.BlockSpec((None, d, tn), lambda l, j: (l, 0, j)),
            pl.BlockSpec((None, 1, tn), lambda l, j: (l, 0, j)),
        ],
        out_specs=pl.BlockSpec((None, SUBLANES, tn), lambda l, j: (l, 0, j)),
        out_shape=jax.ShapeDtypeStruct((n_layers, SUBLANES, n), F32),
        compiler_params=_params(2),
        name="mod_vectors",
    )(c_rows, w_mod, b_mod.reshape(n_layers, 1, n))


def _rms(x, g):
    return x * lax.rsqrt(jnp.mean(x * x, axis=-1, keepdims=True) + NORM_EPS) * g


def _adaln_kernel(xl_ref, xc_ref, g_ref, shift_ref, scale_ref, o_ref, *, n_lat):
    def emit(x_ref):
        y = _rms(x_ref[...], g_ref[...])
        o_ref[...] = (y * (1.0 + scale_ref[0]) + shift_ref[0]).astype(o_ref.dtype)

    t = pl.program_id(1)
    pl.when(t < n_lat)(lambda: emit(xl_ref))
    pl.when(t >= n_lat)(lambda: emit(xc_ref))


def _adaln_norm(x_lat, x_ctx, g, mods, idx_shift, idx_scale, with_ctx):
    b, tl, d = x_lat.shape
    tc = x_ctx.shape[1]
    n_lat = tl // ROW_TILE
    n_ctx = tc // ROW_TILE if with_ctx else 0
    n_rows = tl + (tc if with_ctx else 0)

    def mod_row(bi, t):
        return jnp.where(t < n_lat, bi, 2) * N_MOD

    return pl.pallas_call(
        functools.partial(_adaln_kernel, n_lat=n_lat),
        grid=(b, n_lat + n_ctx),
        in_specs=[
            pl.BlockSpec((None, ROW_TILE, d), lambda bi, t: (bi, jnp.minimum(t, n_lat - 1), 0)),
            pl.BlockSpec((None, ROW_TILE, d), lambda bi, t: (bi, jnp.maximum(t - n_lat, 0), 0)),
            pl.BlockSpec((1, d), lambda bi, t: (0, 0)),
            pl.BlockSpec((1, 1, d), lambda bi, t: (mod_row(bi, t) + idx_shift, 0, 0)),
            pl.BlockSpec((1, 1, d), lambda bi, t: (mod_row(bi, t) + idx_scale, 0, 0)),
        ],
        out_specs=pl.BlockSpec((None, ROW_TILE, d), lambda bi, t: (bi, t, 0)),
        out_shape=jax.ShapeDtypeStruct((b, n_rows, d), BF16),
        compiler_params=_params(2),
        name="adaln_norm",
    )(x_lat, x_ctx, g.reshape(1, d), mods, mods)


def _final_norm_kernel(x_ref, g_ref, o_ref):
    o_ref[...] = _rms(x_ref[...], g_ref[...])


def _final_norm(x, g):
    b, t, d = x.shape
    return pl.pallas_call(
        _final_norm_kernel,
        grid=(b, t // ROW_TILE),
        in_specs=[pl.BlockSpec((None, ROW_TILE, d), lambda bi, i: (bi, i, 0)),
                  pl.BlockSpec((1, d), lambda bi, i: (0, 0))],
        out_specs=pl.BlockSpec((None, ROW_TILE, d), lambda bi, i: (bi, i, 0)),
        out_shape=jax.ShapeDtypeStruct((b, t, d), F32),
        compiler_params=_params(2),
        name="final_norm",
    )(x, g.reshape(1, d))


def _mm_kernel(a_ref, b_ref, o_ref):
    o_ref[...] = jnp.dot(a_ref[...], b_ref[...], preferred_element_type=F32).astype(o_ref.dtype)


def _matmul(a, w, tm_pref=768, tn_pref=1024):
    m, k = a.shape
    n = w.shape[1]
    tm = _tile(m, tm_pref, ROW_TILE)
    tn = _tile(n, tn_pref, LANES)
    return pl.pallas_call(
        _mm_kernel,
        grid=(m // tm, n // tn),
        in_specs=[pl.BlockSpec((tm, k), lambda i, j: (i, 0)),
                  pl.BlockSpec((k, tn), lambda i, j: (0, j))],
        out_specs=pl.BlockSpec((tm, tn), lambda i, j: (i, j)),
        out_shape=jax.ShapeDtypeStruct((m, n), BF16),
        compiler_params=_params(2),
        name="matmul",
    )(a, w)


def _swiglu_kernel(a_ref, wg_ref, wu_ref, o_ref):
    a = a_ref[...]
    g = jnp.dot(a, wg_ref[...], preferred_element_type=F32)
    u = jnp.dot(a, wu_ref[...], preferred_element_type=F32)
    o_ref[...] = (g * jax.nn.sigmoid(g) * u).astype(o_ref.dtype)


def _swiglu_in(h, w_in, tm_pref=768, tn_pref=512):
    m, k = h.shape
    f = w_in.shape[1] // 2
    tm = _tile(m, tm_pref, ROW_TILE)
    tn = _tile(f, tn_pref, LANES)
    nj = f // tn
    return pl.pallas_call(
        _swiglu_kernel,
        grid=(m // tm, nj),
        in_specs=[pl.BlockSpec((tm, k), lambda i, j: (i, 0)),
                  pl.BlockSpec((k, tn), lambda i, j: (0, j)),
                  pl.BlockSpec((k, tn), lambda i, j: (0, j + nj))],
        out_specs=pl.BlockSpec((tm, tn), lambda i, j: (i, j)),
        out_shape=jax.ShapeDtypeStruct((m, f), BF16),
        compiler_params=_params(2),
        name="swiglu_in",
    )(h, w_in, w_in)


def _residual_kernel(a_ref, w_ref, x_ref, gate_ref, o_ref, *, coef):
    y = jnp.dot(a_ref[...], w_ref[...], preferred_element_type=F32)
    gate = gate_ref[0] if coef is None else coef * gate_ref[0]
    o_ref[...] = x_ref[...] + gate * y


def _residual_matmul(a, w, x, mods, idx_gate, coef, a_row_off, is_ctx, tm_pref=1024, tn_pref=512):
    b, r, n = x.shape
    k = a.shape[2]
    tm = _tile(r, tm_pref, ROW_TILE)
    tn = _tile(n, tn_pref, LANES)
    off = a_row_off // tm

    def gate_row(bi):
        return (2 if is_ctx else bi) * N_MOD + idx_gate

    return pl.pallas_call(
        functools.partial(_residual_kernel, coef=coef),
        grid=(b, r // tm, n // tn),
        in_specs=[pl.BlockSpec((None, tm, k), lambda bi, i, j: (bi, i + off, 0)),
                  pl.BlockSpec((k, tn), lambda bi, i, j: (0, j)),
                  pl.BlockSpec((None, tm, tn), lambda bi, i, j: (bi, i, j)),
                  pl.BlockSpec((1, 1, tn), lambda bi, i, j: (gate_row(bi), 0, j))],
        out_specs=pl.BlockSpec((None, tm, tn), lambda bi, i, j: (bi, i, j)),
        out_shape=jax.ShapeDtypeStruct((b, r, n), F32),
        compiler_params=_params(3),
        name="residual_matmul",
    )(a, w, x, mods)


def _merge_kernel(a0_ref, a1_ref, a2_ref, w0_ref, w1_ref, w2_ref, g0_ref, g1_ref, g2_ref, o_ref):
    def branch(a_ref, w_ref, g_ref):
        y = jnp.dot(a_ref[...], w_ref[...], preferred_element_type=F32)
        return jax.nn.sigmoid(g_ref[...].astype(F32)) * y

    acc = branch(a0_ref, w0_ref, g0_ref) + branch(a1_ref, w1_ref, g1_ref)
    o_ref[...] = (acc + branch(a2_ref, w2_ref, g2_ref)).astype(o_ref.dtype)


def _merge(o_gqa, o_mla, y_lru, proj, w0, w1, w2, gate_col, rows, mix_row_off, tm_pref=512, tn_pref=512):
    b = o_gqa.shape[0]
    n = w0.shape[1]
    tm = _tile(rows, tm_pref, ROW_TILE)
    tn = _tile(n, tn_pref, LANES)
    off = mix_row_off // tm
    gcol = gate_col // tn
    nj = n // tn

    def a_spec(a, row_off):
        return pl.BlockSpec((None, tm, a.shape[2]), lambda bi, i, j: (bi, i + row_off, 0))

    def w_spec(w):
        return pl.BlockSpec((w.shape[0], tn), lambda bi, i, j: (0, j))

    def gate_spec(br):
        return pl.BlockSpec((None, tm, tn), lambda bi, i, j: (bi, i + off, gcol + br * nj + j))

    return pl.pallas_call(
        _merge_kernel,
        grid=(b, rows // tm, nj),
        in_specs=[a_spec(o_gqa, 0), a_spec(o_mla, 0), a_spec(y_lru, off), w_spec(w0), w_spec(w1), w_spec(w2),
                  gate_spec(0), gate_spec(1), gate_spec(2)],
        out_specs=pl.BlockSpec((None, tm, tn), lambda bi, i, j: (bi, i, j)),
        out_shape=jax.ShapeDtypeStruct((b, rows, n), BF16),
        compiler_params=_params(3),
        name="merge",
    )(o_gqa, o_mla, y_lru, w0, w1, w2, proj, proj, proj)


def _rope_tables(n_tok, rot_dim, grid_w, ctx_len):
    rows = n_tok // grid_w
    r_idx, c_idx = jnp.meshgrid(jnp.arange(rows), jnp.arange(grid_w), indexing='ij')
    r_idx = r_idx.reshape(-1).astype(F32)
    c_idx = c_idx.reshape(-1).astype(F32)
    n_pairs = rot_dim // 4
    freqs = jnp.power(ROPE_THETA, -jnp.arange(n_pairs, dtype=F32) / n_pairs)
    ang = jnp.concatenate([r_idx[:, None] * freqs, c_idx[:, None] * freqs], axis=-1)
    cos, sin = jnp.cos(ang), jnp.sin(ang)
    cos_full = jnp.concatenate([cos, cos], axis=-1)
    sin_signed = jnp.concatenate([-sin, sin], axis=-1)
    reps = LANES // rot_dim
    cos_full = jnp.tile(cos_full, (1, reps))
    sin_signed = jnp.tile(sin_signed, (1, reps))
    cos_full = jnp.concatenate([cos_full, jnp.ones((ctx_len, LANES), F32)], axis=0)
    sin_signed = jnp.concatenate([sin_signed, jnp.zeros((ctx_len, LANES), F32)], axis=0)
    return cos_full, sin_signed


def _rope128(x, cos, sin):
    return x * cos + pltpu.roll(x, LANES // 2, 1) * sin


def _rope64(x, cos, sin):
    lane = lax.broadcasted_iota(jnp.int32, x.shape, 1)
    first_half = (lane % 64) < 32
    partner = jnp.where(first_half, pltpu.roll(x, LANES - 32, 1), pltpu.roll(x, 32, 1))
    return x * cos + partner * sin


def _prep_kernel(k_ref, cq_ref, ckv_ref, kr_ref, cg_ref, sg_ref, cm_ref, sm_ref,
                 gk_ref, gq_ref, gkv_ref, ko_ref, cqo_ref, ckvo_ref, kro_ref, *, kv_heads):
    cos_g, sin_g = cg_ref[...], sg_ref[...]
    for h in range(kv_heads):
        cols = slice(h * LANES, (h + 1) * LANES)
        y = _rms(k_ref[:, cols].astype(F32), gk_ref[...])
        ko_ref[:, cols] = _rope128(y, cos_g, sin_g).astype(BF16)
    cqo_ref[...] = _rms(cq_ref[...].astype(F32), gq_ref[...]).astype(BF16)
    ckvo_ref[...] = _rms(ckv_ref[...].astype(F32), gkv_ref[...]).astype(BF16)
    kro_ref[...] = _rope64(kr_ref[...].astype(F32), cm_ref[...], sm_ref[...]).astype(BF16)


def _prep(proj, lay, dm, tabs_g, tabs_m, gk, gq, gkv):
    b, tt, _ = proj.shape
    kw = dm.gqa_kv * LANES
    assert lay.k % kw == 0 and lay.cq % dm.q_rank == 0 and lay.ckv % dm.kv_rank == 0
    row = lambda bi, t: (bi, t, 0)
    tab = pl.BlockSpec((ROW_TILE, LANES), lambda bi, t: (t, 0))

    def vec(n):
        return pl.BlockSpec((1, n), lambda bi, t: (0, 0))

    return pl.pallas_call(
        functools.partial(_prep_kernel, kv_heads=dm.gqa_kv),
        grid=(b, tt // ROW_TILE),
        in_specs=[pl.BlockSpec((None, ROW_TILE, kw), lambda bi, t: (bi, t, lay.k // kw)),
                  pl.BlockSpec((None, ROW_TILE, dm.q_rank), lambda bi, t: (bi, t, lay.cq // dm.q_rank)),
                  pl.BlockSpec((None, ROW_TILE, dm.kv_rank), lambda bi, t: (bi, t, lay.ckv // dm.kv_rank)),
                  pl.BlockSpec((None, ROW_TILE, LANES), lambda bi, t: (bi, t, lay.krope // LANES)),
                  tab, tab, tab, tab, vec(LANES), vec(dm.q_rank), vec(dm.kv_rank)],
        out_specs=[pl.BlockSpec((None, ROW_TILE, kw), row),
                   pl.BlockSpec((None, ROW_TILE, dm.q_rank), row),
                   pl.BlockSpec((None, ROW_TILE, dm.kv_rank), row),
                   pl.BlockSpec((None, ROW_TILE, LANES), row)],
        out_shape=[jax.ShapeDtypeStruct((b, tt, kw), BF16),
                   jax.ShapeDtypeStruct((b, tt, dm.q_rank), BF16),
                   jax.ShapeDtypeStruct((b, tt, dm.kv_rank), BF16),
                   jax.ShapeDtypeStruct((b, tt, LANES), BF16)],
        compiler_params=_params(2),
        name="prep",
    )(proj, proj, proj, proj, tabs_g[0], tabs_g[1], tabs_m[0], tabs_m[1],
      gk.reshape(1, LANES), gq.reshape(1, dm.q_rank), gkv.reshape(1, dm.kv_rank))


def _flash(q_ref, load_k, load_v, n_keys, tk, m_ref, l_ref, acc_ref):
    m_ref[...] = jnp.full(m_ref.shape, -jnp.inf, F32)
    l_ref[...] = jnp.zeros(l_ref.shape, F32)
    acc_ref[...] = jnp.zeros(acc_ref.shape, F32)

    def step(start, size):
        s = lax.dot_general(q_ref[...], load_k(start, size), (((1,), (1,)), ((), ())),
                            preferred_element_type=F32)
        m_prev = m_ref[...]
        m_new = jnp.maximum(m_prev, jnp.max(s, axis=1, keepdims=True))
        p = jnp.exp(s - m_new)
        alpha = jnp.exp(m_prev - m_new)
        l_ref[...] = alpha * l_ref[...] + jnp.sum(p, axis=1, keepdims=True)
        acc_ref[...] = alpha * acc_ref[...] + jnp.dot(p.astype(BF16), load_v(start, size),
                                                     preferred_element_type=F32)
        m_ref[...] = m_new

    n_full = n_keys // tk

    def body(c, carry):
        step(pl.multiple_of(c * tk, tk), tk)
        return carry

    lax.fori_loop(0, n_full, body, 0)
    if n_keys % tk:
        step(n_full * tk, n_keys % tk)
    acc_ref[...] = acc_ref[...] / l_ref[...]


def _gqa_kernel(q_ref, k_ref, v_ref, cos_ref, sin_ref, gq_ref, o_ref, qs_ref, m_ref, l_ref, acc_ref,
                *, group, tq, tk, scale):
    cos, sin = cos_ref[...], sin_ref[...]
    for g in range(group):
        y = _rms(q_ref[:, g * LANES:(g + 1) * LANES].astype(F32), gq_ref[...])
        qs_ref[g * tq:(g + 1) * tq, :] = (_rope128(y, cos, sin) * scale).astype(BF16)
    n_keys = k_ref.shape[0]
    _flash(qs_ref,
           lambda s, n: k_ref[pl.ds(s, n), :],
           lambda s, n: v_ref[pl.ds(s, n), :],
           n_keys, min(tk, n_keys), m_ref, l_ref, acc_ref)
    for g in range(group):
        o_ref[:, g * LANES:(g + 1) * LANES] = acc_ref[g * tq:(g + 1) * tq, :].astype(o_ref.dtype)


def _gqa_attention(proj, k_prep, lay, dm, tabs, gq, q_rows, q_off, key_rows, key_off, tq_pref=256, tk=512):
    b = proj.shape[0]
    group = dm.gqa_heads // dm.gqa_kv
    gw = group * LANES
    tq = _tile(q_rows, tq_pref, ROW_TILE)
    assert q_off % tq == 0 and key_off % key_rows == 0
    qo, ko, vcol = q_off // tq, key_off // key_rows, lay.v // LANES
    return pl.pallas_call(
        functools.partial(_gqa_kernel, group=group, tq=tq, tk=tk, scale=float(LANES) ** -0.5),
        grid=(b, dm.gqa_kv, q_rows // tq),
        in_specs=[pl.BlockSpec((None, tq, gw), lambda bi, h, i: (bi, i + qo, h)),
                  pl.BlockSpec((None, key_rows, LANES), lambda bi, h, i: (bi, ko, h)),
                  pl.BlockSpec((None, key_rows, LANES), lambda bi, h, i: (bi, ko, vcol + h)),
                  pl.BlockSpec((tq, LANES), lambda bi, h, i: (i + qo, 0)),
                  pl.BlockSpec((tq, LANES), lambda bi, h, i: (i + qo, 0)),
                  pl.BlockSpec((1, LANES), lambda bi, h, i: (0, 0))],
        out_specs=pl.BlockSpec((None, tq, gw), lambda bi, h, i: (bi, i, h)),
        out_shape=jax.ShapeDtypeStruct((b, q_rows, dm.gqa_heads * LANES), BF16),
        scratch_shapes=[pltpu.VMEM((group * tq, LANES), BF16),
                        pltpu.VMEM((group * tq, 1), F32),
                        pltpu.VMEM((group * tq, 1), F32),
                        pltpu.VMEM((group * tq, LANES), F32)],
        compiler_params=_params(3),
        name="gqa_attention",
    )(proj, k_prep, proj, tabs[0], tabs[1], gq.reshape(1, LANES))


def _mla_kernel(qn_ref, qr_ref, kn_ref, kr_ref, v_ref, cos_ref, sin_ref, o_ref,
                qs_ref, m_ref, l_ref, acc_ref, *, tk, scale):
    qs_ref[:, :LANES] = (qn_ref[...].astype(F32) * scale).astype(BF16)
    qr = _rope64(qr_ref[...].astype(F32), cos_ref[...], sin_ref[...])
    qs_ref[:, LANES:] = (qr * scale).astype(BF16)
    n_keys = kn_ref.shape[0]

    def load_k(s, n):
        return jnp.concatenate([kn_ref[pl.ds(s, n), :], kr_ref[pl.ds(s, n), :]], axis=1)

    _flash(qs_ref, load_k, lambda s, n: v_ref[pl.ds(s, n), :],
           n_keys, min(tk, n_keys), m_ref, l_ref, acc_ref)
    o_ref[...] = acc_ref[...].astype(o_ref.dtype)


def _mla_attention(q_up, kv_up, k_rope, dm, tabs, q_rows, q_off, key_rows, key_off, tq_pref=512, tk=512):
    b = q_up.shape[0]
    h_n = dm.mla_heads
    tq = _tile(q_rows, tq_pref, ROW_TILE)
    assert q_off % tq == 0 and key_off % key_rows == 0
    qo, ko = q_off // tq, key_off // key_rows
    scale = float(LANES + dm.rope_dim) ** -0.5
    return pl.pallas_call(
        functools.partial(_mla_kernel, tk=tk, scale=scale),
        grid=(b, h_n, q_rows // tq),
        in_specs=[pl.BlockSpec((None, tq, LANES), lambda bi, h, i: (bi, i + qo, h)),
                  pl.BlockSpec((None, tq, LANES), lambda bi, h, i: (bi, i + qo, h_n + h)),
                  pl.BlockSpec((None, key_rows, LANES), lambda bi, h, i: (bi, ko, h)),
                  pl.BlockSpec((None, key_rows, LANES), lambda bi, h, i: (bi, ko, 0)),
                  pl.BlockSpec((None, key_rows, LANES), lambda bi, h, i: (bi, ko, h_n + h)),
                  pl.BlockSpec((tq, LANES), lambda bi, h, i: (i + qo, 0)),
                  pl.BlockSpec((tq, LANES), lambda bi, h, i: (i + qo, 0))],
        out_specs=pl.BlockSpec((None, tq, LANES), lambda bi, h, i: (bi, i, h)),
        out_shape=jax.ShapeDtypeStruct((b, q_rows, h_n * LANES), BF16),
        scratch_shapes=[pltpu.VMEM((tq, 2 * LANES), BF16),
                        pltpu.VMEM((tq, 1), F32),
                        pltpu.VMEM((tq, 1), F32),
                        pltpu.VMEM((tq, LANES), F32)],
        compiler_params=_params(3),
        name="mla_attention",
    )(q_up, q_up, kv_up, k_rope, kv_up, tabs[0], tabs[1])


def _lru_kernel(lx_ref, lg_ref, cw_ref, cb_ref, wa_ref, ba_ref, wi_ref, bi_ref, lam_ref, o_ref,
                xs_ref, xc_ref, hf_ref, *, tl, tc, ch):
    tt = tl + tc
    n_lat, n_ctx = tl // ch, tc // ch
    pad = SUBLANES
    zeros = jnp.zeros((pad, LANES), F32)
    xs_ref[0:pad, :] = zeros
    xs_ref[pad + tt:2 * pad + tt, :] = zeros

    def stage(c, carry):
        r = pl.multiple_of(c * ch, ch)
        xs_ref[pl.ds(pad + r, ch), :] = lx_ref[pl.ds(r, ch), :].astype(F32)
        return carry

    lax.fori_loop(0, n_lat + n_ctx, stage, 0)

    def conv(c, carry):
        r = pl.multiple_of(c * ch, ch)
        ext = xs_ref[pl.ds(r, ch + 2 * pad), :]
        n_ext = ch + 2 * pad
        t = r + lax.broadcasted_iota(jnp.int32, (ch, LANES), 0)
        seg_lo = jnp.where(t >= tl, tl, 0)
        seg_hi = jnp.where(t >= tl, tt, tl)
        y = cb_ref[...]
        for j in range(CONV_WIDTH):
            shift = j - 1
            tap = pltpu.roll(ext, (-shift) % n_ext, 0)[pad:pad + ch]
            valid = jnp.where(t + shift >= seg_lo, t + shift, seg_hi) < seg_hi
            y = y + jnp.where(valid, tap, 0.0) * cw_ref[j:j + 1, :]
        xc_ref[pl.ds(r, ch), :] = y
        return carry

    lax.fori_loop(0, n_lat + n_ctx, conv, 0)

    row = lax.broadcasted_iota(jnp.int32, (SUBLANES, LANES), 0)

    def coeffs(xc, d):
        xb = xc.astype(BF16)
        r = jax.nn.sigmoid(jnp.dot(xb, wa_ref[d].astype(BF16), preferred_element_type=F32) + ba_ref[d:d + 1, :])
        i = jax.nn.sigmoid(jnp.dot(xb, wi_ref[d].astype(BF16), preferred_element_type=F32) + bi_ref[d:d + 1, :])
        z = -lam_ref[d:d + 1, :]
        softplus = jnp.maximum(z, 0.0) + jnp.log1p(jnp.exp(-jnp.abs(z)))
        log_a = -LRU_C * r * softplus
        a = jnp.exp(log_a)
        u = jnp.sqrt(1.0 - a * a) * (i * xc)
        return a, u

    def scan_chunk(c, h, d, emit):
        r = pl.multiple_of(c * ch, ch)
        a, u = coeffs(xc_ref[pl.ds(r, ch), :], d)
        n_blk = ch // SUBLANES
        order = range(n_blk) if d == 0 else range(n_blk - 1, -1, -1)
        states = [None] * n_blk
        for j in order:
            aj = a[j * SUBLANES:(j + 1) * SUBLANES]
            uj = u[j * SUBLANES:(j + 1) * SUBLANES]
            for dist in (1, 2, 4):
                if d == 0:
                    shift, mask = dist, row >= dist
                else:
                    shift, mask = SUBLANES - dist, row < SUBLANES - dist
                a_s = jnp.where(mask, pltpu.roll(aj, shift, 0), 1.0)
                u_s = jnp.where(mask, pltpu.roll(uj, shift, 0), 0.0)
                uj = uj + aj * u_s
                aj = aj * a_s
            hj = aj * h + uj
            h = hj[SUBLANES - 1:SUBLANES] if d == 0 else hj[0:1]
            states[j] = hj
        emit(r, jnp.concatenate(states, axis=0))
        return h

    def emit_fwd(start, hs):
        hf_ref[pl.ds(start, ch), :] = hs

    def emit_bwd(start, hs):
        total = hf_ref[pl.ds(start, ch), :] + hs
        gate = jax.nn.gelu(lg_ref[pl.ds(start, ch), :].astype(F32))
        o_ref[pl.ds(start, ch), :] = (total * gate).astype(o_ref.dtype)

    h0 = jnp.zeros((1, LANES), F32)
    h = lax.fori_loop(0, n_ctx, lambda k, h: scan_chunk(n_lat + k, h, 0, emit_fwd), h0)
    lax.fori_loop(0, n_lat, lambda k, h: scan_chunk(k, h, 0, emit_fwd), h)
    h = lax.fori_loop(0, n_ctx, lambda k, h: scan_chunk(n_lat + n_ctx - 1 - k, h, 1, emit_bwd), h0)
    lax.fori_loop(0, n_lat, lambda k, h: scan_chunk(n_lat - 1 - k, h, 1, emit_bwd), h)


def _lru(proj, lay, dm, tl, conv_w, conv_b, wa, ba, wi, bi, lam):
    b, tt, _ = proj.shape
    nb = dm.lru_w // LANES
    xcol, gcol = lay.lx // LANES, lay.lg // LANES
    vec2 = pl.BlockSpec((2, LANES), lambda bi_, n: (0, n))
    wspec = pl.BlockSpec((2, None, LANES, LANES), lambda bi_, n: (0, n, 0, 0))
    return pl.pallas_call(
        functools.partial(_lru_kernel, tl=tl, tc=tt - tl, ch=ROW_TILE),
        grid=(b, nb),
        in_specs=[pl.BlockSpec((None, tt, LANES), lambda bi_, n: (bi_, 0, xcol + n)),
                  pl.BlockSpec((None, tt, LANES), lambda bi_, n: (bi_, 0, gcol + n)),
                  pl.BlockSpec((CONV_WIDTH, LANES), lambda bi_, n: (0, n)),
                  pl.BlockSpec((1, LANES), lambda bi_, n: (0, n)),
                  wspec, vec2, wspec, vec2, vec2],
        out_specs=pl.BlockSpec((None, tt, LANES), lambda bi_, n: (bi_, 0, n)),
        out_shape=jax.ShapeDtypeStruct((b, tt, dm.lru_w), BF16),
        scratch_shapes=[pltpu.VMEM((tt + 2 * SUBLANES, LANES), F32),
                        pltpu.VMEM((tt, LANES), F32),
                        pltpu.VMEM((tt, LANES), F32)],
        compiler_params=_params(2),
        name="rglru",
    )(proj, proj, conv_w, conv_b.reshape(1, dm.lru_w), wa, ba, wi, bi, lam)


def _layer_weights(l, dm, lay, w_ff1_in, w_ff1_out, w_ff2_in, w_ff2_out, w_in, w_mla_uq, w_mla_ukv,
                   w_o_gqa, w_o_mla, w_o_lru, w_out):
    d = w_in.shape[1]
    real_attn = lay.krope + dm.rope_dim
    w_proj = jnp.concatenate(
        [w_in[l][:, :real_attn].astype(BF16), jnp.zeros((d, lay.lx - real_attn), BF16),
         w_in[l][:, real_attn:].astype(BF16)], axis=1)
    h = dm.mla_heads
    uq = w_mla_uq[l].reshape(dm.q_rank, h, LANES + dm.rope_dim)
    uq_rope = jnp.pad(uq[:, :, LANES:], ((0, 0), (0, 0), (0, LANES - dm.rope_dim)))
    w_uq = jnp.concatenate([uq[:, :, :LANES].reshape(dm.q_rank, h * LANES),
                            uq_rope.reshape(dm.q_rank, h * LANES)], axis=1).astype(BF16)
    ukv = w_mla_ukv[l].reshape(dm.kv_rank, h, 2 * LANES)
    w_ukv = jnp.concatenate([ukv[:, :, :LANES].reshape(dm.kv_rank, h * LANES),
                             ukv[:, :, LANES:].reshape(dm.kv_rank, h * LANES)], axis=1).astype(BF16)
    cast = lambda w: w[l].astype(BF16)
    return dict(ff1_in=cast(w_ff1_in), ff1_out=cast(w_ff1_out), ff2_in=cast(w_ff2_in), ff2_out=cast(w_ff2_out),
                proj=w_proj, uq=w_uq, ukv=w_ukv, o_gqa=cast(w_o_gqa), o_mla=cast(w_o_mla),
                o_lru=cast(w_o_lru), out=cast(w_out))


def _forward(dm, x, c, ctx, c_ctx, w_mod, b_mod, g_norm, w_ff1_in, w_ff1_out, w_ff2_in, w_ff2_out, w_in,
             gqa_q_norm, gqa_k_norm, mla_q_norm, mla_kv_norm, w_mla_uq, w_mla_ukv, lru_conv_w, lru_conv_b,
             lru_wa, lru_ba, lru_wi, lru_bi, lru_lambda, w_o_gqa, w_o_mla, w_o_lru, w_out, final_norm):
    b, tl, d = x.shape
    tc = ctx.shape[1]
    tt = tl + tc
    depth = w_mod.shape[0]
    assert b == 2 and tl % ROW_TILE == 0 and tc % ROW_TILE == 0
    lay = _proj_layout(dm, d)
    tabs_g = _rope_tables(tl, LANES, dm.grid_w, tc)
    tabs_m = _rope_tables(tl, dm.rope_dim, dm.grid_w, tc)

    c_rows = jnp.concatenate([c, c_ctx[None, :], jnp.zeros((SUBLANES - b - 1, d), F32)], axis=0)
    mods_all = _mod_vectors(c_rows, w_mod, b_mod)

    xc = ctx
    for l in range(depth):
        last = l == depth - 1
        w = _layer_weights(l, dm, lay, w_ff1_in, w_ff1_out, w_ff2_in, w_ff2_out, w_in, w_mla_uq, w_mla_ukv,
                           w_o_gqa, w_o_mla, w_o_lru, w_out)
        mods = mods_all[l, :b + 1].reshape((b + 1) * N_MOD, 1, d)

        h1 = _adaln_norm(x, xc, g_norm[l, 0], mods, 0, 1, True)
        act = _swiglu_in(h1.reshape(b * tt, d), w['ff1_in']).reshape(b, tt, dm.d_ff)
        x = _residual_matmul(act, w['ff1_out'], x, mods, 2, 0.5, 0, False)
        xc = _residual_matmul(act, w['ff1_out'], xc, mods, 2, 0.5, tl, True)

        h2 = _adaln_norm(x, xc, g_norm[l, 1], mods, 3, 4, True)
        proj = _matmul(h2.reshape(b * tt, d), w['proj']).reshape(b, tt, lay.total)
        k_gqa, cq_n, ckv_n, k_rope = _prep(proj, lay, dm, tabs_g, tabs_m, gqa_k_norm[l], mla_q_norm[l],
                                           mla_kv_norm[l])
        q_up = _matmul(cq_n.reshape(b * tt, dm.q_rank), w['uq']).reshape(b, tt, -1)
        kv_up = _matmul(ckv_n.reshape(b * tt, dm.kv_rank), w['ukv']).reshape(b, tt, -1)
        y_lru = _lru(proj, lay, dm, tl, lru_conv_w[l], lru_conv_b[l], lru_wa[l], lru_ba[l], lru_wi[l],
                     lru_bi[l], lru_lambda[l])

        o_gqa = _gqa_attention(proj, k_gqa, lay, dm, tabs_g, gqa_q_norm[l], tl, 0, tt, 0)
        o_mla = _mla_attention(q_up, kv_up, k_rope, dm, tabs_m, tl, 0, tt, 0)
        merged = _merge(o_gqa, o_mla, y_lru, proj, w['o_gqa'], w['o_mla'], w['o_lru'], lay.gate, tl, 0)
        x = _residual_matmul(merged, w['out'], x, mods, 5, None, 0, False)
        if not last:
            o_gqa_c = _gqa_attention(proj, k_gqa, lay, dm, tabs_g, gqa_q_norm[l], tc, tl, tc, tl)
            o_mla_c = _mla_attention(q_up, kv_up, k_rope, dm, tabs_m, tc, tl, tc, tl)
            merged_c = _merge(o_gqa_c, o_mla_c, y_lru, proj, w['o_gqa'], w['o_mla'], w['o_lru'],
                              lay.gate, tc, tl)
            xc = _residual_matmul(merged_c, w['out'], xc, mods, 5, None, 0, True)

        h3 = _adaln_norm(x, xc, g_norm[l, 2], mods, 6, 7, not last)
        rows = h3.shape[1]
        act = _swiglu_in(h3.reshape(b * rows, d), w['ff2_in']).reshape(b, rows, dm.d_ff)
        x = _residual_matmul(act, w['ff2_out'], x, mods, 8, 0.5, 0, False)
        if not last:
            xc = _residual_matmul(act, w['ff2_out'], xc, mods, 8, 0.5, tl, True)
    return _final_norm(x, final_norm)


def kernel(x, c, ctx, c_ctx, w_mod, b_mod, g_norm, w_ff1_in, w_ff1_out, w_ff2_in, w_ff2_out, w_in, gqa_q_norm, gqa_k_norm, mla_q_norm, mla_kv_norm, w_mla_uq, w_mla_ukv, lru_conv_w, lru_conv_b, lru_wa, lru_ba, lru_wi, lru_bi, lru_lambda, w_o_gqa, w_o_mla, w_o_lru, w_out, final_norm):
    return _forward(DIMS, x, c, ctx, c_ctx, w_mod, b_mod, g_norm, w_ff1_in, w_ff1_out, w_ff2_in, w_ff2_out,
                    w_in, gqa_q_norm, gqa_k_norm, mla_q_norm, mla_kv_norm, w_mla_uq, w_mla_ukv, lru_conv_w,
                    lru_conv_b, lru_wa, lru_ba, lru_wi, lru_bi, lru_lambda, w_o_gqa, w_o_mla, w_o_lru, w_out,
                    final_norm)
```

```python
import dataclasses
import functools

import jax
import jax.numpy as jnp
from jax import lax
from jax.experimental import pallas as pl
from jax.experimental.pallas import tpu as pltpu

F32 = jnp.float32
BF16 = jnp.bfloat16

LANES = 128
SUBLANES = 8
VMEM_LIMIT = 56 * 1024 * 1024

ROPE_THETA = 10000.0
NORM_EPS = 1e-6
N_MOD = 9
LRU_C = 8.0
CONV_WIDTH = 4
ROW_TILE = 256


@dataclasses.dataclass(frozen=True)
class Dims:
    grid_w: int = 64
    gqa_heads: int = 16
    gqa_kv: int = 4
    mla_heads: int = 16
    q_rank: int = 1536
    kv_rank: int = 512
    rope_dim: int = 64
    lru_w: int = 2048
    d_ff: int = 6144
    proj_pad: int = 1024


DIMS = Dims()


def _params(n_grid):
    return pltpu.CompilerParams(dimension_semantics=("arbitrary",) * n_grid,
                                vmem_limit_bytes=VMEM_LIMIT)


def _tile(dim, pref, align):
    best = None
    t = align
    while t <= min(dim, pref):
        if dim % t == 0:
            best = t
        t += align
    return best if best is not None else dim


@dataclasses.dataclass(frozen=True)
class ProjLayout:
    q: int
    k: int
    v: int
    cq: int
    ckv: int
    krope: int
    lx: int
    lg: int
    gate: int
    total: int


def _proj_layout(dm, d_model):
    q = 0
    k = q + dm.gqa_heads * LANES
    v = k + dm.gqa_kv * LANES
    cq = v + dm.gqa_kv * LANES
    ckv = cq + dm.q_rank
    krope = ckv + dm.kv_rank
    attn_end = krope + LANES
    lx = -(-attn_end // dm.proj_pad) * dm.proj_pad
    lg = lx + dm.lru_w
    gate = lg + dm.lru_w
    total = gate + 3 * d_model
    return ProjLayout(q, k, v, cq, ckv, krope, lx, lg, gate, total)


def _mod_kernel(c_ref, w_ref, b_ref, o_ref):
    c = c_ref[...]
    a = (c * jax.nn.sigmoid(c)).astype(BF16)
    acc = jnp.dot(a, w_ref[...].astype(BF16), preferred_element_type=F32)
    o_ref[...] = acc + b_ref[...]


def _mod_vectors(c_rows, w_mod, b_mod):
    n_layers, d, n = w_mod.shape
    tn = _tile(n, 512, LANES)
    return pl.pallas_call(
        _mod_kernel,
        grid=(n_layers, n // tn),
        in_specs=[
            pl.BlockSpec((SUBLANES, d), lambda l, j: (0, 0)),
            pl.BlockSpec((None, d, tn), lambda l, j: (l, 0, j)),
            pl.BlockSpec((None, 1, tn), lambda l, j: (l, 0, j)),
        ],
        out_specs=pl.BlockSpec((None, SUBLANES, tn), lambda l, j: (l, 0, j)),
        out_shape=jax.ShapeDtypeStruct((n_layers, SUBLANES, n), F32),
        compiler_params=_params(2),
        name="mod_vectors",
    )(c_rows, w_mod, b_mod.reshape(n_layers, 1, n))


def _rms(x, g):
    return x * lax.rsqrt(jnp.mean(x * x, axis=-1, keepdims=True) + NORM_EPS) * g


def _adaln_kernel(xl_ref, xc_ref, g_ref, shift_ref, scale_ref, o_ref, *, n_lat):
    def emit(x_ref):
        y = _rms(x_ref[...], g_ref[...])
        o_ref[...] = (y * (1.0 + scale_ref[0]) + shift_ref[0]).astype(o_ref.dtype)

    t = pl.program_id(1)
    pl.when(t < n_lat)(lambda: emit(xl_ref))
    pl.when(t >= n_lat)(lambda: emit(xc_ref))


def _adaln_norm(x_lat, x_ctx, g, mods, idx_shift, idx_scale, with_ctx):
    b, tl, d = x_lat.shape
    tc = x_ctx.shape[1]
    n_lat = tl // ROW_TILE
    n_ctx = tc // ROW_TILE if with_ctx else 0
    n_rows = tl + (tc if with_ctx else 0)

    def mod_row(bi, t):
        return jnp.where(t < n_lat, bi, 2) * N_MOD

    return pl.pallas_call(
        functools.partial(_adaln_kernel, n_lat=n_lat),
        grid=(b, n_lat + n_ctx),
        in_specs=[
            pl.BlockSpec((None, ROW_TILE, d), lambda bi, t: (bi, jnp.minimum(t, n_lat - 1), 0)),
            pl.BlockSpec((None, ROW_TILE, d), lambda bi, t: (bi, jnp.maximum(t - n_lat, 0), 0)),
            pl.BlockSpec((1, d), lambda bi, t: (0, 0)),
            pl.BlockSpec((1, 1, d), lambda bi, t: (mod_row(bi, t) + idx_shift, 0, 0)),
            pl.BlockSpec((1, 1, d), lambda bi, t: (mod_row(bi, t) + idx_scale, 0, 0)),
        ],
        out_specs=pl.BlockSpec((None, ROW_TILE, d), lambda bi, t: (bi, t, 0)),
        out_shape=jax.ShapeDtypeStruct((b, n_rows, d), BF16),
        compiler_params=_params(2),
        name="adaln_norm",
    )(x_lat, x_ctx, g.reshape(1, d), mods, mods)


def _final_norm_kernel(x_ref, g_ref, o_ref):
    o_ref[...] = _rms(x_ref[...], g_ref[...])


def _final_norm(x, g):
    b, t, d = x.shape
    return pl.pallas_call(
        _final_norm_kernel,
        grid=(b, t // ROW_TILE),
        in_specs=[pl.BlockSpec((None, ROW_TILE, d), lambda bi, i: (bi, i, 0)),
                  pl.BlockSpec((1, d), lambda bi, i: (0, 0))],
        out_specs=pl.BlockSpec((None, ROW_TILE, d), lambda bi, i: (bi, i, 0)),
        out_shape=jax.ShapeDtypeStruct((b, t, d), F32),
        compiler_params=_params(2),
        name="final_norm",
    )(x, g.reshape(1, d))


def _mm_kernel(a_ref, b_ref, o_ref):
    o_ref[...] = jnp.dot(a_ref[...], b_ref[...], preferred_element_type=F32).astype(o_ref.dtype)


def _matmul(a, w, tm_pref=768, tn_pref=1024):
    m, k = a.shape
    n = w.shape[1]
    tm = _tile(m, tm_pref, ROW_TILE)
    tn = _tile(n, tn_pref, LANES)
    return pl.pallas_call(
        _mm_kernel,
        grid=(m // tm, n // tn),
        in_specs=[pl.BlockSpec((tm, k), lambda i, j: (i, 0)),
                  pl.BlockSpec((k, tn), lambda i, j: (0, j))],
        out_specs=pl.BlockSpec((tm, tn), lambda i, j: (i, j)),
        out_shape=jax.ShapeDtypeStruct((m, n), BF16),
        compiler_params=_params(2),
        name="matmul",
    )(a, w)


def _swiglu_kernel(a_ref, wg_ref, wu_ref, o_ref):
    a = a_ref[...]
    g = jnp.dot(a, wg_ref[...], preferred_element_type=F32)
    u = jnp.dot(a, wu_ref[...], preferred_element_type=F32)
    o_ref[...] = (g * jax.nn.sigmoid(g) * u).astype(o_ref.dtype)


def _swiglu_in(h, w_in, tm_pref=768, tn_pref=512):
    m, k = h.shape
    f = w_in.shape[1] // 2
    tm = _tile(m, tm_pref, ROW_TILE)
    tn = _tile(f, tn_pref, LANES)
    nj = f // tn
    return pl.pallas_call(
        _swiglu_kernel,
        grid=(m // tm, nj),
        in_specs=[pl.BlockSpec((tm, k), lambda i, j: (i, 0)),
                  pl.BlockSpec((k, tn), lambda i, j: (0, j)),
                  pl.BlockSpec((k, tn), lambda i, j: (0, j + nj))],
        out_specs=pl.BlockSpec((tm, tn), lambda i, j: (i, j)),
        out_shape=jax.ShapeDtypeStruct((m, f), BF16),
        compiler_params=_params(2),
        name="swiglu_in",
    )(h, w_in, w_in)


def _residual_kernel(a_ref, w_ref, x_ref, gate_ref, o_ref, *, coef):
    y = jnp.dot(a_ref[...], w_ref[...], preferred_element_type=F32)
    gate = gate_ref[0] if coef is None else coef * gate_ref[0]
    o_ref[...] = x_ref[...] + gate * y


def _residual_matmul(a, w, x, mods, idx_gate, coef, a_row_off, is_ctx, tm_pref=1024, tn_pref=512):
    b, r, n = x.shape
    k = a.shape[2]
    tm = _tile(r, tm_pref, ROW_TILE)
    tn = _tile(n, tn_pref, LANES)
    off = a_row_off // tm

    def gate_row(bi):
        return (2 if is_ctx else bi) * N_MOD + idx_gate

    return pl.pallas_call(
        functools.partial(_residual_kernel, coef=coef),
        grid=(b, r // tm, n // tn),
        in_specs=[pl.BlockSpec((None, tm, k), lambda bi, i, j: (bi, i + off, 0)),
                  pl.BlockSpec((k, tn), lambda bi, i, j: (0, j)),
                  pl.BlockSpec((None, tm, tn), lambda bi, i, j: (bi, i, j)),
                  pl.BlockSpec((1, 1, tn), lambda bi, i, j: (gate_row(bi), 0, j))],
        out_specs=pl.BlockSpec((None, tm, tn), lambda bi, i, j: (bi, i, j)),
        out_shape=jax.ShapeDtypeStruct((b, r, n), F32),
        compiler_params=_params(3),
        name="residual_matmul",
    )(a, w, x, mods)


def _merge_kernel(a0_ref, a1_ref, a2_ref, w0_ref, w1_ref, w2_ref, g0_ref, g1_ref, g2_ref, o_ref):
    def branch(a_ref, w_ref, g_ref):
        y = jnp.dot(a_ref[...], w_ref[...], preferred_element_type=F32)
        return jax.nn.sigmoid(g_ref[...].astype(F32)) * y

    acc = branch(a0_ref, w0_ref, g0_ref) + branch(a1_ref, w1_ref, g1_ref)
    o_ref[...] = (acc + branch(a2_ref, w2_ref, g2_ref)).astype(o_ref.dtype)


def _merge(o_gqa, o_mla, y_lru, proj, w0, w1, w2, gate_col, rows, mix_row_off, tm_pref=512, tn_pref=512):
    b = o_gqa.shape[0]
    n = w0.shape[1]
    tm = _tile(rows, tm_pref, ROW_TILE)
    tn = _tile(n, tn_pref, LANES)
    off = mix_row_off // tm
    gcol = gate_col // tn
    nj = n // tn

    def a_spec(a, row_off):
        return pl.BlockSpec((None, tm, a.shape[2]), lambda bi, i, j: (bi, i + row_off, 0))

    def w_spec(w):
        return pl.BlockSpec((w.shape[0], tn), lambda bi, i, j: (0, j))

    def gate_spec(br):
        return pl.BlockSpec((None, tm, tn), lambda bi, i, j: (bi, i + off, gcol + br * nj + j))

    return pl.pallas_call(
        _merge_kernel,
        grid=(b, rows // tm, nj),
        in_specs=[a_spec(o_gqa, 0), a_spec(o_mla, 0), a_spec(y_lru, off), w_spec(w0), w_spec(w1), w_spec(w2),
                  gate_spec(0), gate_spec(1), gate_spec(2)],
        out_specs=pl.BlockSpec((None, tm, tn), lambda bi, i, j: (bi, i, j)),
        out_shape=jax.ShapeDtypeStruct((b, rows, n), BF16),
        compiler_params=_params(3),
        name="merge",
    )(o_gqa, o_mla, y_lru, w0, w1, w2, proj, proj, proj)


def _rope_tables(n_tok, rot_dim, grid_w, ctx_len):
    rows = n_tok // grid_w
    r_idx, c_idx = jnp.meshgrid(jnp.arange(rows), jnp.arange(grid_w), indexing='ij')
    r_idx = r_idx.reshape(-1).astype(F32)
    c_idx = c_idx.reshape(-1).astype(F32)
    n_pairs = rot_dim // 4
    freqs = jnp.power(ROPE_THETA, -jnp.arange(n_pairs, dtype=F32) / n_pairs)
    ang = jnp.concatenate([r_idx[:, None] * freqs, c_idx[:, None] * freqs], axis=-1)
    cos, sin = jnp.cos(ang), jnp.sin(ang)
    cos_full = jnp.concatenate([cos, cos], axis=-1)
    sin_signed = jnp.concatenate([-sin, sin], axis=-1)
    reps = LANES // rot_dim
    cos_full = jnp.tile(cos_full, (1, reps))
    sin_signed = jnp.tile(sin_signed, (1, reps))
    cos_full = jnp.concatenate([cos_full, jnp.ones((ctx_len, LANES), F32)], axis=0)
    sin_signed = jnp.concatenate([sin_signed, jnp.zeros((ctx_len, LANES), F32)], axis=0)
    return cos_full, sin_signed


def _rope128(x, cos, sin):
    return x * cos + pltpu.roll(x, LANES // 2, 1) * sin


def _rope64(x, cos, sin):
    lane = lax.broadcasted_iota(jnp.int32, x.shape, 1)
    first_half = (lane % 64) < 32
    partner = jnp.where(first_half, pltpu.roll(x, LANES - 32, 1), pltpu.roll(x, 32, 1))
    return x * cos + partner * sin


def _prep_kernel(k_ref, cq_ref, ckv_ref, kr_ref, cg_ref, sg_ref, cm_ref, sm_ref,
                 gk_ref, gq_ref, gkv_ref, ko_ref, cqo_ref, ckvo_ref, kro_ref, *, kv_heads):
    cos_g, sin_g = cg_ref[...], sg_ref[...]
    for h in range(kv_heads):
        cols = slice(h * LANES, (h + 1) * LANES)
        y = _rms(k_ref[:, cols].astype(F32), gk_ref[...])
        ko_ref[:, cols] = _rope128(y, cos_g, sin_g).astype(BF16)
    cqo_ref[...] = _rms(cq_ref[...].astype(F32), gq_ref[...]).astype(BF16)
    ckvo_ref[...] = _rms(ckv_ref[...].astype(F32), gkv_ref[...]).astype(BF16)
    kro_ref[...] = _rope64(kr_ref[...].astype(F32), cm_ref[...], sm_ref[...]).astype(BF16)


def _prep(proj, lay, dm, tabs_g, tabs_m, gk, gq, gkv):
    b, tt, _ = proj.shape
    kw = dm.gqa_kv * LANES
    assert lay.k % kw == 0 and lay.cq % dm.q_rank == 0 and lay.ckv % dm.kv_rank == 0
    row = lambda bi, t: (bi, t, 0)
    tab = pl.BlockSpec((ROW_TILE, LANES), lambda bi, t: (t, 0))

    def vec(n):
        return pl.BlockSpec((1, n), lambda bi, t: (0, 0))

    return pl.pallas_call(
        functools.partial(_prep_kernel, kv_heads=dm.gqa_kv),
        grid=(b, tt // ROW_TILE),
        in_specs=[pl.BlockSpec((None, ROW_TILE, kw), lambda bi, t: (bi, t, lay.k // kw)),
                  pl.BlockSpec((None, ROW_TILE, dm.q_rank), lambda bi, t: (bi, t, lay.cq // dm.q_rank)),
                  pl.BlockSpec((None, ROW_TILE, dm.kv_rank), lambda bi, t: (bi, t, lay.ckv // dm.kv_rank)),
                  pl.BlockSpec((None, ROW_TILE, LANES), lambda bi, t: (bi, t, lay.krope // LANES)),
                  tab, tab, tab, tab, vec(LANES), vec(dm.q_rank), vec(dm.kv_rank)],
        out_specs=[pl.BlockSpec((None, ROW_TILE, kw), row),
                   pl.BlockSpec((None, ROW_TILE, dm.q_rank), row),
                   pl.BlockSpec((None, ROW_TILE, dm.kv_rank), row),
                   pl.BlockSpec((None, ROW_TILE, LANES), row)],
        out_shape=[jax.ShapeDtypeStruct((b, tt, kw), BF16),
                   jax.ShapeDtypeStruct((b, tt, dm.q_rank), BF16),
                   jax.ShapeDtypeStruct((b, tt, dm.kv_rank), BF16),
                   jax.ShapeDtypeStruct((b, tt, LANES), BF16)],
        compiler_params=_params(2),
        name="prep",
    )(proj, proj, proj, proj, tabs_g[0], tabs_g[1], tabs_m[0], tabs_m[1],
      gk.reshape(1, LANES), gq.reshape(1, dm.q_rank), gkv.reshape(1, dm.kv_rank))


LOG2E = 1.4426950408889634
KEY_CHUNK = 768


def _flash_t(qt_ref, load_k, load_vt, n_chunks, m_ref, l_ref, acc_ref):
    m_ref[...] = jnp.full(m_ref.shape, -jnp.inf, F32)
    l_ref[...] = jnp.zeros(l_ref.shape, F32)
    acc_ref[...] = jnp.zeros(acc_ref.shape, F32)

    n_q = qt_ref.shape[1]
    n_split = 2 if n_q % (2 * LANES) == 0 else 1
    width = n_q // n_split

    def body(c, carry):
        k, vt = load_k(c), load_vt(c)
        scores = [jnp.dot(k, qt_ref[:, h * width:(h + 1) * width], preferred_element_type=F32)
                  for h in range(n_split)]
        for h in range(n_split):
            cols = slice(h * width, (h + 1) * width)
            s = scores[h]
            m_prev = m_ref[:, cols]
            m_new = jnp.maximum(m_prev, jnp.max(s, axis=0, keepdims=True))
            p = jnp.exp2(s - m_new)
            alpha = jnp.exp2(m_prev - m_new)
            l_ref[:, cols] = alpha * l_ref[:, cols] + jnp.sum(p, axis=0, keepdims=True)
            acc_ref[:, cols] = alpha * acc_ref[:, cols] + jnp.dot(vt, p.astype(BF16),
                                                                   preferred_element_type=F32)
            m_ref[:, cols] = m_new
        return carry

    lax.fori_loop(0, n_chunks, body, 0)
    return acc_ref[...] / l_ref[...]


def _stage_values_t(v_ref, vt_ref, tk):
    @pl.when(pl.program_id(2) == 0)
    def _():
        for c in range(vt_ref.shape[0]):
            vt_ref[c] = v_ref[c * tk:(c + 1) * tk, :].astype(F32).T.astype(BF16)


def _gqa_kernel(q_ref, k_ref, v_ref, cos_ref, sin_ref, gq_ref, o_ref, qt_ref, vt_ref, m_ref, l_ref, acc_ref,
                *, group, tq, tk, scale):
    _stage_values_t(v_ref, vt_ref, tk)
    cos, sin = cos_ref[...], sin_ref[...]
    for g in range(group):
        y = _rms(q_ref[:, g * LANES:(g + 1) * LANES].astype(F32), gq_ref[...])
        qt_ref[:, g * tq:(g + 1) * tq] = (_rope128(y, cos, sin) * scale).T.astype(BF16)
    out_t = _flash_t(qt_ref,
                     lambda c: k_ref[pl.ds(pl.multiple_of(c * tk, tk), tk), :],
                     lambda c: vt_ref[c],
                     vt_ref.shape[0], m_ref, l_ref, acc_ref)
    for g in range(group):
        o_ref[:, g * LANES:(g + 1) * LANES] = out_t[:, g * tq:(g + 1) * tq].T.astype(o_ref.dtype)


def _gqa_attention(proj, k_prep, lay, dm, tabs, gq, q_rows, q_off, key_rows, key_off, tq_pref=256):
    b = proj.shape[0]
    group = dm.gqa_heads // dm.gqa_kv
    gw = group * LANES
    tq = _tile(q_rows, tq_pref, ROW_TILE)
    tk = _tile(key_rows, KEY_CHUNK, ROW_TILE)
    assert q_off % tq == 0 and key_off % key_rows == 0
    qo, ko, vcol = q_off // tq, key_off // key_rows, lay.v // LANES
    return pl.pallas_call(
        functools.partial(_gqa_kernel, group=group, tq=tq, tk=tk, scale=float(LANES) ** -0.5 * LOG2E),
        grid=(b, dm.gqa_kv, q_rows // tq),
        in_specs=[pl.BlockSpec((None, tq, gw), lambda bi, h, i: (bi, i + qo, h)),
                  pl.BlockSpec((None, key_rows, LANES), lambda bi, h, i: (bi, ko, h)),
                  pl.BlockSpec((None, key_rows, LANES), lambda bi, h, i: (bi, ko, vcol + h)),
                  pl.BlockSpec((tq, LANES), lambda bi, h, i: (i + qo, 0)),
                  pl.BlockSpec((tq, LANES), lambda bi, h, i: (i + qo, 0)),
                  pl.BlockSpec((1, LANES), lambda bi, h, i: (0, 0))],
        out_specs=pl.BlockSpec((None, tq, gw), lambda bi, h, i: (bi, i, h)),
        out_shape=jax.ShapeDtypeStruct((b, q_rows, dm.gqa_heads * LANES), BF16),
        scratch_shapes=[pltpu.VMEM((LANES, group * tq), BF16),
                        pltpu.VMEM((key_rows // tk, LANES, tk), BF16),
                        pltpu.VMEM((1, group * tq), F32),
                        pltpu.VMEM((1, group * tq), F32),
                        pltpu.VMEM((LANES, group * tq), F32)],
        compiler_params=_params(3),
        name="gqa_attention",
    )(proj, k_prep, proj, tabs[0], tabs[1], gq.reshape(1, LANES))


def _mla_kernel(qn_ref, qr_ref, kn_ref, kr_ref, v_ref, cos_ref, sin_ref, o_ref,
                qt_ref, vt_ref, m_ref, l_ref, acc_ref, *, tk, scale):
    _stage_values_t(v_ref, vt_ref, tk)
    qt_ref[:LANES, :] = (qn_ref[...].astype(F32) * scale).T.astype(BF16)
    qr = _rope64(qr_ref[...].astype(F32), cos_ref[...], sin_ref[...])
    qt_ref[LANES:, :] = (qr * scale).T.astype(BF16)

    def load_k(c):
        rows = pl.ds(pl.multiple_of(c * tk, tk), tk)
        return jnp.concatenate([kn_ref[rows, :], kr_ref[rows, :]], axis=1)

    out_t = _flash_t(qt_ref, load_k, lambda c: vt_ref[c], vt_ref.shape[0], m_ref, l_ref, acc_ref)
    o_ref[...] = out_t.T.astype(o_ref.dtype)


def _mla_attention(q_up, kv_up, k_rope, dm, tabs, q_rows, q_off, key_rows, key_off, tq_pref=1024):
    b = q_up.shape[0]
    h_n = dm.mla_heads
    tq = _tile(q_rows, tq_pref, ROW_TILE)
    tk = _tile(key_rows, KEY_CHUNK, ROW_TILE)
    assert q_off % tq == 0 and key_off % key_rows == 0
    qo, ko = q_off // tq, key_off // key_rows
    scale = float(LANES + dm.rope_dim) ** -0.5 * LOG2E
    return pl.pallas_call(
        functools.partial(_mla_kernel, tk=tk, scale=scale),
        grid=(b, h_n, q_rows // tq),
        in_specs=[pl.BlockSpec((None, tq, LANES), lambda bi, h, i: (bi, i + qo, h)),
                  pl.BlockSpec((None, tq, LANES), lambda bi, h, i: (bi, i + qo, h_n + h)),
                  pl.BlockSpec((None, key_rows, LANES), lambda bi, h, i: (bi, ko, h)),
                  pl.BlockSpec((None, key_rows, LANES), lambda bi, h, i: (bi, ko, 0)),
                  pl.BlockSpec((None, key_rows, LANES), lambda bi, h, i: (bi, ko, h_n + h)),
                  pl.BlockSpec((tq, LANES), lambda bi, h, i: (i + qo, 0)),
                  pl.BlockSpec((tq, LANES), lambda bi, h, i: (i + qo, 0))],
        out_specs=pl.BlockSpec((None, tq, LANES), lambda bi, h, i: (bi, i, h)),
        out_shape=jax.ShapeDtypeStruct((b, q_rows, h_n * LANES), BF16),
        scratch_shapes=[pltpu.VMEM((2 * LANES, tq), BF16),
                        pltpu.VMEM((key_rows // tk, LANES, tk), BF16),
                        pltpu.VMEM((1, tq), F32),
                        pltpu.VMEM((1, tq), F32),
                        pltpu.VMEM((LANES, tq), F32)],
        compiler_params=_params(3),
        name="mla_attention",
    )(q_up, q_up, kv_up, k_rope, kv_up, tabs[0], tabs[1])


def _lru_kernel(lx_ref, lg_ref, cw_ref, cb_ref, wa_ref, ba_ref, wi_ref, bi_ref, lam_ref, o_ref,
                xs_ref, xc_ref, hf_ref, *, tl, tc, ch):
    tt = tl + tc
    n_lat, n_ctx = tl // ch, tc // ch
    pad = SUBLANES
    zeros = jnp.zeros((pad, LANES), F32)
    xs_ref[0:pad, :] = zeros
    xs_ref[pad + tt:2 * pad + tt, :] = zeros

    def stage(c, carry):
        r = pl.multiple_of(c * ch, ch)
        xs_ref[pl.ds(pad + r, ch), :] = lx_ref[pl.ds(r, ch), :].astype(F32)
        return carry

    lax.fori_loop(0, n_lat + n_ctx, stage, 0)

    def conv(c, carry):
        r = pl.multiple_of(c * ch, ch)
        ext = xs_ref[pl.ds(r, ch + 2 * pad), :]
        n_ext = ch + 2 * pad
        t = r + lax.broadcasted_iota(jnp.int32, (ch, LANES), 0)
        seg_lo = jnp.where(t >= tl, tl, 0)
        seg_hi = jnp.where(t >= tl, tt, tl)
        y = cb_ref[...]
        for j in range(CONV_WIDTH):
            shift = j - 1
            tap = pltpu.roll(ext, (-shift) % n_ext, 0)[pad:pad + ch]
            valid = jnp.where(t + shift >= seg_lo, t + shift, seg_hi) < seg_hi
            y = y + jnp.where(valid, tap, 0.0) * cw_ref[j:j + 1, :]
        xc_ref[pl.ds(r, ch), :] = y
        return carry

    lax.fori_loop(0, n_lat + n_ctx, conv, 0)

    row = lax.broadcasted_iota(jnp.int32, (SUBLANES, LANES), 0)

    def coeffs(xc, d):
        xb = xc.astype(BF16)
        r = jax.nn.sigmoid(jnp.dot(xb, wa_ref[d].astype(BF16), preferred_element_type=F32) + ba_ref[d:d + 1, :])
        i = jax.nn.sigmoid(jnp.dot(xb, wi_ref[d].astype(BF16), preferred_element_type=F32) + bi_ref[d:d + 1, :])
        z = -lam_ref[d:d + 1, :]
        softplus = jnp.maximum(z, 0.0) + jnp.log1p(jnp.exp(-jnp.abs(z)))
        log_a = -LRU_C * r * softplus
        a = jnp.exp(log_a)
        u = jnp.sqrt(1.0 - a * a) * (i * xc)
        return a, u

    def scan_chunk(c, h, d, emit):
        r = pl.multiple_of(c * ch, ch)
        a, u = coeffs(xc_ref[pl.ds(r, ch), :], d)
        n_blk = ch // SUBLANES
        order = range(n_blk) if d == 0 else range(n_blk - 1, -1, -1)
        states = [None] * n_blk
        for j in order:
            aj = a[j * SUBLANES:(j + 1) * SUBLANES]
            uj = u[j * SUBLANES:(j + 1) * SUBLANES]
            for dist in (1, 2, 4):
                if d == 0:
                    shift, mask = dist, row >= dist
                else:
                    shift, mask = SUBLANES - dist, row < SUBLANES - dist
                a_s = jnp.where(mask, pltpu.roll(aj, shift, 0), 1.0)
                u_s = jnp.where(mask, pltpu.roll(uj, shift, 0), 0.0)
                uj = uj + aj * u_s
                aj = aj * a_s
            hj = aj * h + uj
            h = hj[SUBLANES - 1:SUBLANES] if d == 0 else hj[0:1]
            states[j] = hj
        emit(r, jnp.concatenate(states, axis=0))
        return h

    def emit_fwd(start, hs):
        hf_ref[pl.ds(start, ch), :] = hs

    def emit_bwd(start, hs):
        total = hf_ref[pl.ds(start, ch), :] + hs
        gate = jax.nn.gelu(lg_ref[pl.ds(start, ch), :].astype(F32))
        o_ref[pl.ds(start, ch), :] = (total * gate).astype(o_ref.dtype)

    h0 = jnp.zeros((1, LANES), F32)
    h = lax.fori_loop(0, n_ctx, lambda k, h: scan_chunk(n_lat + k, h, 0, emit_fwd), h0)
    lax.fori_loop(0, n_lat, lambda k, h: scan_chunk(k, h, 0, emit_fwd), h)
    h = lax.fori_loop(0, n_ctx, lambda k, h: scan_chunk(n_lat + n_ctx - 1 - k, h, 1, emit_bwd), h0)
    lax.fori_loop(0, n_lat, lambda k, h: scan_chunk(n_lat - 1 - k, h, 1, emit_bwd), h)


def _lru(proj, lay, dm, tl, conv_w, conv_b, wa, ba, wi, bi, lam):
    b, tt, _ = proj.shape
    nb = dm.lru_w // LANES
    xcol, gcol = lay.lx // LANES, lay.lg // LANES
    vec2 = pl.BlockSpec((2, LANES), lambda bi_, n: (0, n))
    wspec = pl.BlockSpec((2, None, LANES, LANES), lambda bi_, n: (0, n, 0, 0))
    return pl.pallas_call(
        functools.partial(_lru_kernel, tl=tl, tc=tt - tl, ch=ROW_TILE),
        grid=(b, nb),
        in_specs=[pl.BlockSpec((None, tt, LANES), lambda bi_, n: (bi_, 0, xcol + n)),
                  pl.BlockSpec((None, tt, LANES), lambda bi_, n: (bi_, 0, gcol + n)),
                  pl.BlockSpec((CONV_WIDTH, LANES), lambda bi_, n: (0, n)),
                  pl.BlockSpec((1, LANES), lambda bi_, n: (0, n)),
                  wspec, vec2, wspec, vec2, vec2],
        out_specs=pl.BlockSpec((None, tt, LANES), lambda bi_, n: (bi_, 0, n)),
        out_shape=jax.ShapeDtypeStruct((b, tt, dm.lru_w), BF16),
        scratch_shapes=[pltpu.VMEM((tt + 2 * SUBLANES, LANES), F32),
                        pltpu.VMEM((tt, LANES), F32),
                        pltpu.VMEM((tt, LANES), F32)],
        compiler_params=_params(2),
        name="rglru",
    )(proj, proj, conv_w, conv_b.reshape(1, dm.lru_w), wa, ba, wi, bi, lam)


def _layer_weights(l, dm, lay, w_ff1_in, w_ff1_out, w_ff2_in, w_ff2_out, w_in, w_mla_uq, w_mla_ukv,
                   w_o_gqa, w_o_mla, w_o_lru, w_out):
    d = w_in.shape[1]
    real_attn = lay.krope + dm.rope_dim
    w_proj = jnp.concatenate(
        [w_in[l][:, :real_attn].astype(BF16), jnp.zeros((d, lay.lx - real_attn), BF16),
         w_in[l][:, real_attn:].astype(BF16)], axis=1)
    h = dm.mla_heads
    uq = w_mla_uq[l].reshape(dm.q_rank, h, LANES + dm.rope_dim)
    uq_rope = jnp.pad(uq[:, :, LANES:], ((0, 0), (0, 0), (0, LANES - dm.rope_dim)))
    w_uq = jnp.concatenate([uq[:, :, :LANES].reshape(dm.q_rank, h * LANES),
                            uq_rope.reshape(dm.q_rank, h * LANES)], axis=1).astype(BF16)
    ukv = w_mla_ukv[l].reshape(dm.kv_rank, h, 2 * LANES)
    w_ukv = jnp.concatenate([ukv[:, :, :LANES].reshape(dm.kv_rank, h * LANES),
                             ukv[:, :, LANES:].reshape(dm.kv_rank, h * LANES)], axis=1).astype(BF16)
    cast = lambda w: w[l].astype(BF16)
    return dict(ff1_in=cast(w_ff1_in), ff1_out=cast(w_ff1_out), ff2_in=cast(w_ff2_in), ff2_out=cast(w_ff2_out),
                proj=w_proj, uq=w_uq, ukv=w_ukv, o_gqa=cast(w_o_gqa), o_mla=cast(w_o_mla),
                o_lru=cast(w_o_lru), out=cast(w_out))


def _forward(dm, x, c, ctx, c_ctx, w_mod, b_mod, g_norm, w_ff1_in, w_ff1_out, w_ff2_in, w_ff2_out, w_in,
             gqa_q_norm, gqa_k_norm, mla_q_norm, mla_kv_norm, w_mla_uq, w_mla_ukv, lru_conv_w, lru_conv_b,
             lru_wa, lru_ba, lru_wi, lru_bi, lru_lambda, w_o_gqa, w_o_mla, w_o_lru, w_out, final_norm):
    b, tl, d = x.shape
    tc = ctx.shape[1]
    tt = tl + tc
    depth = w_mod.shape[0]
    assert b == 2 and tl % ROW_TILE == 0 and tc % ROW_TILE == 0
    lay = _proj_layout(dm, d)
    tabs_g = _rope_tables(tl, LANES, dm.grid_w, tc)
    tabs_m = _rope_tables(tl, dm.rope_dim, dm.grid_w, tc)

    c_rows = jnp.concatenate([c, c_ctx[None, :], jnp.zeros((SUBLANES - b - 1, d), F32)], axis=0)
    mods_all = _mod_vectors(c_rows, w_mod, b_mod)

    xc = ctx
    for l in range(depth):
        last = l == depth - 1
        w = _layer_weights(l, dm, lay, w_ff1_in, w_ff1_out, w_ff2_in, w_ff2_out, w_in, w_mla_uq, w_mla_ukv,
                           w_o_gqa, w_o_mla, w_o_lru, w_out)
        mods = mods_all[l, :b + 1].reshape((b + 1) * N_MOD, 1, d)

        h1 = _adaln_norm(x, xc, g_norm[l, 0], mods, 0, 1, True)
        act = _swiglu_in(h1.reshape(b * tt, d), w['ff1_in']).reshape(b, tt, dm.d_ff)
        x = _residual_matmul(act, w['ff1_out'], x, mods, 2, 0.5, 0, False)
        xc = _residual_matmul(act, w['ff1_out'], xc, mods, 2, 0.5, tl, True)

        h2 = _adaln_norm(x, xc, g_norm[l, 1], mods, 3, 4, True)
        proj = _matmul(h2.reshape(b * tt, d), w['proj']).reshape(b, tt, lay.total)
        k_gqa, cq_n, ckv_n, k_rope = _prep(proj, lay, dm, tabs_g, tabs_m, gqa_k_norm[l], mla_q_norm[l],
                                           mla_kv_norm[l])
        q_up = _matmul(cq_n.reshape(b * tt, dm.q_rank), w['uq']).reshape(b, tt, -1)
        kv_up = _matmul(ckv_n.reshape(b * tt, dm.kv_rank), w['ukv']).reshape(b, tt, -1)
        y_lru = _lru(proj, lay, dm, tl, lru_conv_w[l], lru_conv_b[l], lru_wa[l], lru_ba[l], lru_wi[l],
                     lru_bi[l], lru_lambda[l])

        o_gqa = _gqa_attention(proj, k_gqa, lay, dm, tabs_g, gqa_q_norm[l], tl, 0, tt, 0)
        o_mla = _mla_attention(q_up, kv_up, k_rope, dm, tabs_m, tl, 0, tt, 0)
        merged = _merge(o_gqa, o_mla, y_lru, proj, w['o_gqa'], w['o_mla'], w['o_lru'], lay.gate, tl, 0)
        x = _residual_matmul(merged, w['out'], x, mods, 5, None, 0, False)
        if not last:
            o_gqa_c = _gqa_attention(proj, k_gqa, lay, dm, tabs_g, gqa_q_norm[l], tc, tl, tc, tl)
            o_mla_c = _mla_attention(q_up, kv_up, k_rope, dm, tabs_m, tc, tl, tc, tl)
            merged_c = _merge(o_gqa_c, o_mla_c, y_lru, proj, w['o_gqa'], w['o_mla'], w['o_lru'],
                              lay.gate, tc, tl)
            xc = _residual_matmul(merged_c, w['out'], xc, mods, 5, None, 0, True)

        h3 = _adaln_norm(x, xc, g_norm[l, 2], mods, 6, 7, not last)
        rows = h3.shape[1]
        act = _swiglu_in(h3.reshape(b * rows, d), w['ff2_in']).reshape(b, rows, dm.d_ff)
        x = _residual_matmul(act, w['ff2_out'], x, mods, 8, 0.5, 0, False)
        if not last:
            xc = _residual_matmul(act, w['ff2_out'], xc, mods, 8, 0.5, tl, True)
    return _final_norm(x, final_norm)


def kernel(x, c, ctx, c_ctx, w_mod, b_mod, g_norm, w_ff1_in, w_ff1_out, w_ff2_in, w_ff2_out, w_in, gqa_q_norm, gqa_k_norm, mla_q_norm, mla_kv_norm, w_mla_uq, w_mla_ukv, lru_conv_w, lru_conv_b, lru_wa, lru_ba, lru_wi, lru_bi, lru_lambda, w_o_gqa, w_o_mla, w_o_lru, w_out, final_norm):
    return _forward(DIMS, x, c, ctx, c_ctx, w_mod, b_mod, g_norm, w_ff1_in, w_ff1_out, w_ff2_in, w_ff2_out,
                    w_in, gqa_q_norm, gqa_k_norm, mla_q_norm, mla_kv_norm, w_mla_uq, w_mla_ukv, lru_conv_w,
                    lru_conv_b, lru_wa, lru_ba, lru_wi, lru_bi, lru_lambda, w_o_gqa, w_o_mla, w_o_lru, w_out,
                    final_norm)
```

```python
import dataclasses
import functools

import jax
import jax.numpy as jnp
from jax import lax
from jax.experimental import pallas as pl
from jax.experimental.pallas import tpu as pltpu

F32 = jnp.float32
BF16 = jnp.bfloat16

LANES = 128
SUBLANES = 8
VMEM_LIMIT = 60 * 1024 * 1024

ROPE_THETA = 10000.0
NORM_EPS = 1e-6
N_MOD = 9
LRU_C = 8.0
CONV_WIDTH = 4
ROW_TILE = 256


@dataclasses.dataclass(frozen=True)
class Dims:
    grid_w: int = 64
    gqa_heads: int = 16
    gqa_kv: int = 4
    mla_heads: int = 16
    q_rank: int = 1536
    kv_rank: int = 512
    rope_dim: int = 64
    lru_w: int = 2048
    d_ff: int = 6144
    proj_pad: int = 1024


DIMS = Dims()


def _params(n_grid):
    return pltpu.CompilerParams(dimension_semantics=("arbitrary",) * n_grid,
                                vmem_limit_bytes=VMEM_LIMIT)


def _tile(dim, pref, align):
    best = None
    t = align
    while t <= min(dim, pref):
        if dim % t == 0:
            best = t
        t += align
    return best if best is not None else dim


@dataclasses.dataclass(frozen=True)
class ProjLayout:
    q: int
    k: int
    v: int
    cq: int
    ckv: int
    krope: int
    lx: int
    lg: int
    gate: int
    total: int


def _proj_layout(dm, d_model):
    q = 0
    k = q + dm.gqa_heads * LANES
    v = k + dm.gqa_kv * LANES
    cq = v + dm.gqa_kv * LANES
    ckv = cq + dm.q_rank
    krope = ckv + dm.kv_rank
    attn_end = krope + LANES
    lx = -(-attn_end // dm.proj_pad) * dm.proj_pad
    lg = lx + dm.lru_w
    gate = lg + dm.lru_w
    total = gate + 3 * d_model
    return ProjLayout(q, k, v, cq, ckv, krope, lx, lg, gate, total)


def _mod_kernel(c_ref, w_ref, b_ref, o_ref):
    c = c_ref[...]
    a = (c * jax.nn.sigmoid(c)).astype(BF16)
    acc = jnp.dot(a, w_ref[...].astype(BF16), preferred_element_type=F32)
    o_ref[...] = acc + b_ref[...]


def _mod_vectors(c_rows, w_mod, b_mod):
    n_layers, d, n = w_mod.shape
    tn = _tile(n, 512, LANES)
    return pl.pallas_call(
        _mod_kernel,
        grid=(n_layers, n // tn),
        in_specs=[
            pl.BlockSpec((SUBLANES, d), lambda l, j: (0, 0)),
            pl.BlockSpec((None, d, tn), lambda l, j: (l, 0, j)),
            pl.BlockSpec((None, 1, tn), lambda l, j: (l, 0, j)),
        ],
        out_specs=pl.BlockSpec((None, SUBLANES, tn), lambda l, j: (l, 0, j)),
        out_shape=jax.ShapeDtypeStruct((n_layers, SUBLANES, n), F32),
        compiler_params=_params(2),
        name="mod_vectors",
    )(c_rows, w_mod, b_mod.reshape(n_layers, 1, n))


def _rms(x, g):
    return x * lax.rsqrt(jnp.mean(x * x, axis=-1, keepdims=True) + NORM_EPS) * g


def _adaln_kernel(xl_ref, xc_ref, g_ref, shift_ref, scale_ref, o_ref, *, n_lat):
    def emit(x_ref):
        y = _rms(x_ref[...], g_ref[...])
        o_ref[...] = (y * (1.0 + scale_ref[0]) + shift_ref[0]).astype(o_ref.dtype)

    t = pl.program_id(1)
    pl.when(t < n_lat)(lambda: emit(xl_ref))
    pl.when(t >= n_lat)(lambda: emit(xc_ref))


def _adaln_norm(x_lat, x_ctx, g, mods, idx_shift, idx_scale, with_ctx):
    b, tl, d = x_lat.shape
    tc = x_ctx.shape[1]
    n_lat = tl // ROW_TILE
    n_ctx = tc // ROW_TILE if with_ctx else 0
    n_rows = tl + (tc if with_ctx else 0)

    def mod_row(bi, t):
        return jnp.where(t < n_lat, bi, 2) * N_MOD

    return pl.pallas_call(
        functools.partial(_adaln_kernel, n_lat=n_lat),
        grid=(b, n_lat + n_ctx),
        in_specs=[
            pl.BlockSpec((None, ROW_TILE, d), lambda bi, t: (bi, jnp.minimum(t, n_lat - 1), 0)),
            pl.BlockSpec((None, ROW_TILE, d), lambda bi, t: (bi, jnp.maximum(t - n_lat, 0), 0)),
            pl.BlockSpec((1, d), lambda bi, t: (0, 0)),
            pl.BlockSpec((1, 1, d), lambda bi, t: (mod_row(bi, t) + idx_shift, 0, 0)),
            pl.BlockSpec((1, 1, d), lambda bi, t: (mod_row(bi, t) + idx_scale, 0, 0)),
        ],
        out_specs=pl.BlockSpec((None, ROW_TILE, d), lambda bi, t: (bi, t, 0)),
        out_shape=jax.ShapeDtypeStruct((b, n_rows, d), BF16),
        compiler_params=_params(2),
        name="adaln_norm",
    )(x_lat, x_ctx, g.reshape(1, d), mods, mods)


def _final_norm_kernel(x_ref, g_ref, o_ref):
    o_ref[...] = _rms(x_ref[...], g_ref[...])


def _final_norm(x, g):
    b, t, d = x.shape
    return pl.pallas_call(
        _final_norm_kernel,
        grid=(b, t // ROW_TILE),
        in_specs=[pl.BlockSpec((None, ROW_TILE, d), lambda bi, i: (bi, i, 0)),
                  pl.BlockSpec((1, d), lambda bi, i: (0, 0))],
        out_specs=pl.BlockSpec((None, ROW_TILE, d), lambda bi, i: (bi, i, 0)),
        out_shape=jax.ShapeDtypeStruct((b, t, d), F32),
        compiler_params=_params(2),
        name="final_norm",
    )(x, g.reshape(1, d))


def _mm_kernel(a_ref, b_ref, o_ref):
    o_ref[...] = jnp.dot(a_ref[...], b_ref[...], preferred_element_type=F32).astype(o_ref.dtype)


def _matmul(a, w, tm_pref=1536, tn_pref=512):
    m, k = a.shape
    n = w.shape[1]
    tm = _tile(m, tm_pref, ROW_TILE)
    tn = _tile(n, tn_pref, LANES)
    return pl.pallas_call(
        _mm_kernel,
        grid=(m // tm, n // tn),
        in_specs=[pl.BlockSpec((tm, k), lambda i, j: (i, 0)),
                  pl.BlockSpec((k, tn), lambda i, j: (0, j))],
        out_specs=pl.BlockSpec((tm, tn), lambda i, j: (i, j)),
        out_shape=jax.ShapeDtypeStruct((m, n), BF16),
        compiler_params=_params(2),
        name="matmul",
    )(a, w)


def _swiglu_kernel(a_ref, wg_ref, wu_ref, o_ref):
    a = a_ref[...]
    g = jnp.dot(a, wg_ref[...], preferred_element_type=F32)
    u = jnp.dot(a, wu_ref[...], preferred_element_type=F32)
    o_ref[...] = (g * jax.nn.sigmoid(g) * u).astype(o_ref.dtype)


def _swiglu_in(h, w_in, tm_pref=1536, tn_pref=512):
    m, k = h.shape
    f = w_in.shape[1] // 2
    tm = _tile(m, tm_pref, ROW_TILE)
    tn = _tile(f, tn_pref, LANES)
    nj = f // tn
    return pl.pallas_call(
        _swiglu_kernel,
        grid=(m // tm, nj),
        in_specs=[pl.BlockSpec((tm, k), lambda i, j: (i, 0)),
                  pl.BlockSpec((k, tn), lambda i, j: (0, j)),
                  pl.BlockSpec((k, tn), lambda i, j: (0, j + nj))],
        out_specs=pl.BlockSpec((tm, tn), lambda i, j: (i, j)),
        out_shape=jax.ShapeDtypeStruct((m, f), BF16),
        compiler_params=_params(2),
        name="swiglu_in",
    )(h, w_in, w_in)


def _residual_kernel(a_ref, w_ref, x_ref, gate_ref, o_ref, *, coef):
    y = jnp.dot(a_ref[...], w_ref[...], preferred_element_type=F32)
    gate = gate_ref[0] if coef is None else coef * gate_ref[0]
    o_ref[...] = x_ref[...] + gate * y


def _residual_matmul(a, w, x, mods, idx_gate, coef, a_row_off, is_ctx, tm_pref=1024, tn_pref=512):
    b, r, n = x.shape
    k = a.shape[2]
    tm = _tile(r, tm_pref, ROW_TILE)
    tn = _tile(n, tn_pref, LANES)
    off = a_row_off // tm

    def gate_row(bi):
        return (2 if is_ctx else bi) * N_MOD + idx_gate

    return pl.pallas_call(
        functools.partial(_residual_kernel, coef=coef),
        grid=(b, r // tm, n // tn),
        in_specs=[pl.BlockSpec((None, tm, k), lambda bi, i, j: (bi, i + off, 0)),
                  pl.BlockSpec((k, tn), lambda bi, i, j: (0, j)),
                  pl.BlockSpec((None, tm, tn), lambda bi, i, j: (bi, i, j)),
                  pl.BlockSpec((1, 1, tn), lambda bi, i, j: (gate_row(bi), 0, j))],
        out_specs=pl.BlockSpec((None, tm, tn), lambda bi, i, j: (bi, i, j)),
        out_shape=jax.ShapeDtypeStruct((b, r, n), F32),
        compiler_params=_params(3),
        name="residual_matmul",
    )(a, w, x, mods)


def _merge_kernel(a0_ref, a1_ref, a2_ref, w0_ref, w1_ref, w2_ref, g0_ref, g1_ref, g2_ref, o_ref):
    def branch(a_ref, w_ref, g_ref):
        y = jnp.dot(a_ref[...], w_ref[...], preferred_element_type=F32)
        return jax.nn.sigmoid(g_ref[...].astype(F32)) * y

    acc = branch(a0_ref, w0_ref, g0_ref) + branch(a1_ref, w1_ref, g1_ref)
    o_ref[...] = (acc + branch(a2_ref, w2_ref, g2_ref)).astype(o_ref.dtype)


def _merge(o_gqa, o_mla, y_lru, proj, w0, w1, w2, gate_col, rows, mix_row_off, tm_pref=1024, tn_pref=512):
    b = o_gqa.shape[0]
    n = w0.shape[1]
    tm = _tile(rows, tm_pref, ROW_TILE)
    tn = _tile(n, tn_pref, LANES)
    off = mix_row_off // tm
    gcol = gate_col // tn
    nj = n // tn

    def a_spec(a, row_off):
        return pl.BlockSpec((None, tm, a.shape[2]), lambda bi, i, j: (bi, i + row_off, 0))

    def w_spec(w):
        return pl.BlockSpec((w.shape[0], tn), lambda bi, i, j: (0, j))

    def gate_spec(br):
        return pl.BlockSpec((None, tm, tn), lambda bi, i, j: (bi, i + off, gcol + br * nj + j))

    return pl.pallas_call(
        _merge_kernel,
        grid=(b, rows // tm, nj),
        in_specs=[a_spec(o_gqa, 0), a_spec(o_mla, 0), a_spec(y_lru, off), w_spec(w0), w_spec(w1), w_spec(w2),
                  gate_spec(0), gate_spec(1), gate_spec(2)],
        out_specs=pl.BlockSpec((None, tm, tn), lambda bi, i, j: (bi, i, j)),
        out_shape=jax.ShapeDtypeStruct((b, rows, n), BF16),
        compiler_params=_params(3),
        name="merge",
    )(o_gqa, o_mla, y_lru, w0, w1, w2, proj, proj, proj)


def _rope_tables(n_tok, rot_dim, grid_w, ctx_len):
    rows = n_tok // grid_w
    r_idx, c_idx = jnp.meshgrid(jnp.arange(rows), jnp.arange(grid_w), indexing='ij')
    r_idx = r_idx.reshape(-1).astype(F32)
    c_idx = c_idx.reshape(-1).astype(F32)
    n_pairs = rot_dim // 4
    freqs = jnp.power(ROPE_THETA, -jnp.arange(n_pairs, dtype=F32) / n_pairs)
    ang = jnp.concatenate([r_idx[:, None] * freqs, c_idx[:, None] * freqs], axis=-1)
    cos, sin = jnp.cos(ang), jnp.sin(ang)
    cos_full = jnp.concatenate([cos, cos], axis=-1)
    sin_signed = jnp.concatenate([-sin, sin], axis=-1)
    reps = LANES // rot_dim
    cos_full = jnp.tile(cos_full, (1, reps))
    sin_signed = jnp.tile(sin_signed, (1, reps))
    cos_full = jnp.concatenate([cos_full, jnp.ones((ctx_len, LANES), F32)], axis=0)
    sin_signed = jnp.concatenate([sin_signed, jnp.zeros((ctx_len, LANES), F32)], axis=0)
    return cos_full, sin_signed


def _rope128(x, cos, sin):
    return x * cos + pltpu.roll(x, LANES // 2, 1) * sin


def _rope64(x, cos, sin):
    lane = lax.broadcasted_iota(jnp.int32, x.shape, 1)
    first_half = (lane % 64) < 32
    partner = jnp.where(first_half, pltpu.roll(x, LANES - 32, 1), pltpu.roll(x, 32, 1))
    return x * cos + partner * sin


def _prep_kernel(k_ref, cq_ref, ckv_ref, kr_ref, cg_ref, sg_ref, cm_ref, sm_ref,
                 gk_ref, gq_ref, gkv_ref, ko_ref, cqo_ref, ckvo_ref, kro_ref, *, kv_heads):
    cos_g, sin_g = cg_ref[...], sg_ref[...]
    for h in range(kv_heads):
        cols = slice(h * LANES, (h + 1) * LANES)
        y = _rms(k_ref[:, cols].astype(F32), gk_ref[...])
        ko_ref[:, cols] = _rope128(y, cos_g, sin_g).astype(BF16)
    cqo_ref[...] = _rms(cq_ref[...].astype(F32), gq_ref[...]).astype(BF16)
    ckvo_ref[...] = _rms(ckv_ref[...].astype(F32), gkv_ref[...]).astype(BF16)
    kro_ref[...] = _rope64(kr_ref[...].astype(F32), cm_ref[...], sm_ref[...]).astype(BF16)


def _prep(proj, lay, dm, tabs_g, tabs_m, gk, gq, gkv):
    b, tt, _ = proj.shape
    kw = dm.gqa_kv * LANES
    assert lay.k % kw == 0 and lay.cq % dm.q_rank == 0 and lay.ckv % dm.kv_rank == 0
    row = lambda bi, t: (bi, t, 0)
    tab = pl.BlockSpec((ROW_TILE, LANES), lambda bi, t: (t, 0))

    def vec(n):
        return pl.BlockSpec((1, n), lambda bi, t: (0, 0))

    return pl.pallas_call(
        functools.partial(_prep_kernel, kv_heads=dm.gqa_kv),
        grid=(b, tt // ROW_TILE),
        in_specs=[pl.BlockSpec((None, ROW_TILE, kw), lambda bi, t: (bi, t, lay.k // kw)),
                  pl.BlockSpec((None, ROW_TILE, dm.q_rank), lambda bi, t: (bi, t, lay.cq // dm.q_rank)),
                  pl.BlockSpec((None, ROW_TILE, dm.kv_rank), lambda bi, t: (bi, t, lay.ckv // dm.kv_rank)),
                  pl.BlockSpec((None, ROW_TILE, LANES), lambda bi, t: (bi, t, lay.krope // LANES)),
                  tab, tab, tab, tab, vec(LANES), vec(dm.q_rank), vec(dm.kv_rank)],
        out_specs=[pl.BlockSpec((None, ROW_TILE, kw), row),
                   pl.BlockSpec((None, ROW_TILE, dm.q_rank), row),
                   pl.BlockSpec((None, ROW_TILE, dm.kv_rank), row),
                   pl.BlockSpec((None, ROW_TILE, LANES), row)],
        out_shape=[jax.ShapeDtypeStruct((b, tt, kw), BF16),
                   jax.ShapeDtypeStruct((b, tt, dm.q_rank), BF16),
                   jax.ShapeDtypeStruct((b, tt, dm.kv_rank), BF16),
                   jax.ShapeDtypeStruct((b, tt, LANES), BF16)],
        compiler_params=_params(2),
        name="prep",
    )(proj, proj, proj, proj, tabs_g[0], tabs_g[1], tabs_m[0], tabs_m[1],
      gk.reshape(1, LANES), gq.reshape(1, dm.q_rank), gkv.reshape(1, dm.kv_rank))


LOG2E = 1.4426950408889634
KEY_CHUNK = 768


def _flash_scratch(dk, n_q, key_rows, tk):
    return [pltpu.VMEM((dk, n_q), BF16),
            pltpu.VMEM((key_rows // tk, LANES, tk), BF16),
            pltpu.VMEM((1, n_q), F32),
            pltpu.VMEM((1, n_q), F32),
            pltpu.VMEM((LANES, n_q), F32),
            pltpu.VMEM((tk, n_q), F32),
            pltpu.VMEM((tk, n_q), F32),
            pltpu.VMEM((1, n_q), F32),
            pltpu.VMEM((1, n_q), F32)]


def _flash_t(qt_ref, load_k, load_vt, n_chunks, stats):
    m_ref, l_ref, acc_ref, s0_ref, s1_ref, smax0_ref, smax1_ref = stats
    s_refs, smax_refs = (s0_ref, s1_ref), (smax0_ref, smax1_ref)
    m_ref[...] = jnp.full(m_ref.shape, -jnp.inf, F32)
    l_ref[...] = jnp.zeros(l_ref.shape, F32)
    acc_ref[...] = jnp.zeros(acc_ref.shape, F32)

    def scores(c, slot):
        s = jnp.dot(load_k(c), qt_ref[...], preferred_element_type=F32)
        s_refs[slot][...] = s
        smax_refs[slot][...] = jnp.max(s, axis=0, keepdims=True)

    def update(c, slot):
        m_prev = m_ref[...]
        m_new = jnp.maximum(m_prev, smax_refs[slot][...])
        p = jnp.exp2(s_refs[slot][...] - m_new)
        alpha = jnp.exp2(m_prev - m_new)
        l_ref[...] = alpha * l_ref[...] + jnp.sum(p, axis=0, keepdims=True)
        acc_ref[...] = alpha * acc_ref[...] + jnp.dot(load_vt(c), p.astype(BF16),
                                                     preferred_element_type=F32)
        m_ref[...] = m_new

    scores(0, 0)
    n_pairs = (n_chunks - 1) // 2

    def body(i, carry):
        c = 2 * i
        scores(c + 1, 1)
        update(c, 0)
        scores(c + 2, 0)
        update(c + 1, 1)
        return carry

    lax.fori_loop(0, n_pairs, body, 0)
    if (n_chunks - 1) % 2:
        scores(n_chunks - 1, 1)
        update(n_chunks - 2, 0)
        update(n_chunks - 1, 1)
    else:
        update(n_chunks - 1, 0)
    return acc_ref[...] / l_ref[...]


def _stage_values_t(v_ref, vt_ref, tk):
    @pl.when(pl.program_id(2) == 0)
    def _():
        for c in range(vt_ref.shape[0]):
            vt_ref[c] = v_ref[c * tk:(c + 1) * tk, :].astype(F32).T.astype(BF16)


def _gqa_kernel(q_ref, k_ref, v_ref, cos_ref, sin_ref, gq_ref, o_ref, qt_ref, vt_ref, *stats,
                group, tq, tk, scale):
    _stage_values_t(v_ref, vt_ref, tk)
    cos, sin = cos_ref[...], sin_ref[...]
    for g in range(group):
        y = _rms(q_ref[:, g * LANES:(g + 1) * LANES].astype(F32), gq_ref[...])
        qt_ref[:, g * tq:(g + 1) * tq] = (_rope128(y, cos, sin) * scale).T.astype(BF16)
    out_t = _flash_t(qt_ref,
                     lambda c: k_ref[pl.ds(pl.multiple_of(c * tk, tk), tk), :],
                     lambda c: vt_ref[c],
                     vt_ref.shape[0], stats)
    for g in range(group):
        o_ref[:, g * LANES:(g + 1) * LANES] = out_t[:, g * tq:(g + 1) * tq].T.astype(o_ref.dtype)


def _gqa_attention(proj, k_prep, lay, dm, tabs, gq, q_rows, q_off, key_rows, key_off, tq_pref=256):
    b = proj.shape[0]
    group = dm.gqa_heads // dm.gqa_kv
    gw = group * LANES
    tq = _tile(q_rows, tq_pref, ROW_TILE)
    tk = _tile(key_rows, KEY_CHUNK, ROW_TILE)
    assert q_off % tq == 0 and key_off % key_rows == 0
    qo, ko, vcol = q_off // tq, key_off // key_rows, lay.v // LANES
    return pl.pallas_call(
        functools.partial(_gqa_kernel, group=group, tq=tq, tk=tk, scale=float(LANES) ** -0.5 * LOG2E),
        grid=(b, dm.gqa_kv, q_rows // tq),
        in_specs=[pl.BlockSpec((None, tq, gw), lambda bi, h, i: (bi, i + qo, h)),
                  pl.BlockSpec((None, key_rows, LANES), lambda bi, h, i: (bi, ko, h)),
                  pl.BlockSpec((None, key_rows, LANES), lambda bi, h, i: (bi, ko, vcol + h)),
                  pl.BlockSpec((tq, LANES), lambda bi, h, i: (i + qo, 0)),
                  pl.BlockSpec((tq, LANES), lambda bi, h, i: (i + qo, 0)),
                  pl.BlockSpec((1, LANES), lambda bi, h, i: (0, 0))],
        out_specs=pl.BlockSpec((None, tq, gw), lambda bi, h, i: (bi, i, h)),
        out_shape=jax.ShapeDtypeStruct((b, q_rows, dm.gqa_heads * LANES), BF16),
        scratch_shapes=_flash_scratch(LANES, group * tq, key_rows, tk),
        compiler_params=_params(3),
        name="gqa_attention",
    )(proj, k_prep, proj, tabs[0], tabs[1], gq.reshape(1, LANES))


def _mla_kernel(qn_ref, qr_ref, kn_ref, kr_ref, v_ref, cos_ref, sin_ref, o_ref,
                qt_ref, vt_ref, *stats, tk, scale):
    _stage_values_t(v_ref, vt_ref, tk)
    qt_ref[:LANES, :] = (qn_ref[...].astype(F32) * scale).T.astype(BF16)
    qr = _rope64(qr_ref[...].astype(F32), cos_ref[...], sin_ref[...])
    qt_ref[LANES:, :] = (qr * scale).T.astype(BF16)

    def load_k(c):
        rows = pl.ds(pl.multiple_of(c * tk, tk), tk)
        return jnp.concatenate([kn_ref[rows, :], kr_ref[rows, :]], axis=1)

    out_t = _flash_t(qt_ref, load_k, lambda c: vt_ref[c], vt_ref.shape[0], stats)
    o_ref[...] = out_t.T.astype(o_ref.dtype)


def _mla_attention(q_up, kv_up, k_rope, dm, tabs, q_rows, q_off, key_rows, key_off, tq_pref=1024):
    b = q_up.shape[0]
    h_n = dm.mla_heads
    tq = _tile(q_rows, tq_pref, ROW_TILE)
    tk = _tile(key_rows, KEY_CHUNK, ROW_TILE)
    assert q_off % tq == 0 and key_off % key_rows == 0
    qo, ko = q_off // tq, key_off // key_rows
    scale = float(LANES + dm.rope_dim) ** -0.5 * LOG2E
    return pl.pallas_call(
        functools.partial(_mla_kernel, tk=tk, scale=scale),
        grid=(b, h_n, q_rows // tq),
        in_specs=[pl.BlockSpec((None, tq, LANES), lambda bi, h, i: (bi, i + qo, h)),
                  pl.BlockSpec((None, tq, LANES), lambda bi, h, i: (bi, i + qo, h_n + h)),
                  pl.BlockSpec((None, key_rows, LANES), lambda bi, h, i: (bi, ko, h)),
                  pl.BlockSpec((None, key_rows, LANES), lambda bi, h, i: (bi, ko, 0)),
                  pl.BlockSpec((None, key_rows, LANES), lambda bi, h, i: (bi, ko, h_n + h)),
                  pl.BlockSpec((tq, LANES), lambda bi, h, i: (i + qo, 0)),
                  pl.BlockSpec((tq, LANES), lambda bi, h, i: (i + qo, 0))],
        out_specs=pl.BlockSpec((None, tq, LANES), lambda bi, h, i: (bi, i, h)),
        out_shape=jax.ShapeDtypeStruct((b, q_rows, h_n * LANES), BF16),
        scratch_shapes=_flash_scratch(2 * LANES, tq, key_rows, tk),
        compiler_params=_params(3),
        name="mla_attention",
    )(q_up, q_up, kv_up, k_rope, kv_up, tabs[0], tabs[1])


def _lru_kernel(lx_ref, lg_ref, cw_ref, cb_ref, wa_ref, ba_ref, wi_ref, bi_ref, lam_ref, o_ref,
                xs_ref, xc_ref, hf_ref, hb_ref, *, tl, tc, ch):
    tt = tl + tc
    n_lat, n_ctx = tl // ch, tc // ch
    pad = SUBLANES
    zeros = jnp.zeros((pad, LANES), F32)
    xs_ref[0:pad, :] = zeros
    xs_ref[pad + tt:2 * pad + tt, :] = zeros

    def stage(c, carry):
        r = pl.multiple_of(c * ch, ch)
        xs_ref[pl.ds(pad + r, ch), :] = lx_ref[pl.ds(r, ch), :].astype(F32)
        return carry

    lax.fori_loop(0, n_lat + n_ctx, stage, 0)

    def conv(c, carry):
        r = pl.multiple_of(c * ch, ch)
        ext = xs_ref[pl.ds(r, ch + 2 * pad), :]
        n_ext = ch + 2 * pad
        starts = jnp.logical_or(c == 0, c == n_lat)
        ends = jnp.logical_or(c == n_lat - 1, c == n_lat + n_ctx - 1)
        head = jnp.where(starts, 0.0, ext[:pad])
        tail = jnp.where(ends, 0.0, ext[pad + ch:])
        ext = jnp.concatenate([head, ext[pad:pad + ch], tail], axis=0)
        y = cb_ref[...]
        for j in range(CONV_WIDTH):
            shift = j - 1
            tap = ext if shift == 0 else pltpu.roll(ext, (-shift) % n_ext, 0)
            y = y + tap[pad:pad + ch] * cw_ref[j:j + 1, :]
        xc_ref[pl.ds(r, ch), :] = y
        return carry

    lax.fori_loop(0, n_lat + n_ctx, conv, 0)

    row = lax.broadcasted_iota(jnp.int32, (SUBLANES, LANES), 0)

    def coeffs(xc, d):
        xb = xc.astype(BF16)
        r = jax.nn.sigmoid(jnp.dot(xb, wa_ref[d].astype(BF16), preferred_element_type=F32) + ba_ref[d:d + 1, :])
        i = jax.nn.sigmoid(jnp.dot(xb, wi_ref[d].astype(BF16), preferred_element_type=F32) + bi_ref[d:d + 1, :])
        z = -lam_ref[d:d + 1, :]
        softplus = jnp.maximum(z, 0.0) + jnp.log1p(jnp.exp(-jnp.abs(z)))
        log_a = -LRU_C * r * softplus
        a = jnp.exp(log_a)
        u = jnp.sqrt(1.0 - a * a) * (i * xc)
        return a, u

    def scan_chunk(c, h, d, out_ref):
        r = pl.multiple_of(c * ch, ch)
        a, u = coeffs(xc_ref[pl.ds(r, ch), :], d)
        n_blk = ch // SUBLANES
        order = range(n_blk) if d == 0 else range(n_blk - 1, -1, -1)
        edge = SUBLANES - 1 if d == 0 else 0
        blocks = []
        for j in range(n_blk):
            aj = a[j * SUBLANES:(j + 1) * SUBLANES]
            uj = u[j * SUBLANES:(j + 1) * SUBLANES]
            for dist in (1, 2, 4):
                if d == 0:
                    shift, mask = dist, row >= dist
                else:
                    shift, mask = SUBLANES - dist, row < SUBLANES - dist
                a_s = jnp.where(mask, pltpu.roll(aj, shift, 0), 1.0)
                u_s = jnp.where(mask, pltpu.roll(uj, shift, 0), 0.0)
                uj = uj + aj * u_s
                aj = aj * a_s
            blocks.append((aj, uj))
        carry_in = [None] * n_blk
        for j in order:
            carry_in[j] = h
            aj, uj = blocks[j]
            h = aj[edge:edge + 1] * h + uj[edge:edge + 1]
        states = [blocks[j][0] * carry_in[j] + blocks[j][1] for j in range(n_blk)]
        out_ref[pl.ds(r, ch), :] = jnp.concatenate(states, axis=0)
        return h

    def both(fwd_chunk, bwd_chunk, carry):
        return (scan_chunk(fwd_chunk, carry[0], 0, hf_ref), scan_chunk(bwd_chunk, carry[1], 1, hb_ref))

    h0 = jnp.zeros((1, LANES), F32)
    carry = lax.fori_loop(0, n_ctx, lambda k, cr: both(n_lat + k, n_lat + n_ctx - 1 - k, cr), (h0, h0))
    lax.fori_loop(0, n_lat, lambda k, cr: both(k, n_lat - 1 - k, cr), carry)

    def combine(c, carry):
        rows = pl.ds(pl.multiple_of(c * ch, ch), ch)
        gate = jax.nn.gelu(lg_ref[rows, :].astype(F32))
        o_ref[rows, :] = ((hf_ref[rows, :] + hb_ref[rows, :]) * gate).astype(o_ref.dtype)
        return carry

    lax.fori_loop(0, n_lat + n_ctx, combine, 0)


def _lru(proj, lay, dm, tl, conv_w, conv_b, wa, ba, wi, bi, lam):
    b, tt, _ = proj.shape
    nb = dm.lru_w // LANES
    xcol, gcol = lay.lx // LANES, lay.lg // LANES
    vec2 = pl.BlockSpec((2, LANES), lambda bi_, n: (0, n))
    wspec = pl.BlockSpec((2, None, LANES, LANES), lambda bi_, n: (0, n, 0, 0))
    return pl.pallas_call(
        functools.partial(_lru_kernel, tl=tl, tc=tt - tl, ch=ROW_TILE),
        grid=(b, nb),
        in_specs=[pl.BlockSpec((None, tt, LANES), lambda bi_, n: (bi_, 0, xcol + n)),
                  pl.BlockSpec((None, tt, LANES), lambda bi_, n: (bi_, 0, gcol + n)),
                  pl.BlockSpec((CONV_WIDTH, LANES), lambda bi_, n: (0, n)),
                  pl.BlockSpec((1, LANES), lambda bi_, n: (0, n)),
                  wspec, vec2, wspec, vec2, vec2],
        out_specs=pl.BlockSpec((None, tt, LANES), lambda bi_, n: (bi_, 0, n)),
        out_shape=jax.ShapeDtypeStruct((b, tt, dm.lru_w), BF16),
        scratch_shapes=[pltpu.VMEM((tt + 2 * SUBLANES, LANES), F32),
                        pltpu.VMEM((tt, LANES), F32),
                        pltpu.VMEM((tt, LANES), F32),
                        pltpu.VMEM((tt, LANES), F32)],
        compiler_params=_params(2),
        name="rglru",
    )(proj, proj, conv_w, conv_b.reshape(1, dm.lru_w), wa, ba, wi, bi, lam)


def _layer_weights(l, dm, lay, w_ff1_in, w_ff1_out, w_ff2_in, w_ff2_out, w_in, w_mla_uq, w_mla_ukv,
                   w_o_gqa, w_o_mla, w_o_lru, w_out):
    d = w_in.shape[1]
    real_attn = lay.krope + dm.rope_dim
    w_proj = jnp.concatenate(
        [w_in[l][:, :real_attn].astype(BF16), jnp.zeros((d, lay.lx - real_attn), BF16),
         w_in[l][:, real_attn:].astype(BF16)], axis=1)
    h = dm.mla_heads
    uq = w_mla_uq[l].reshape(dm.q_rank, h, LANES + dm.rope_dim)
    uq_rope = jnp.pad(uq[:, :, LANES:], ((0, 0), (0, 0), (0, LANES - dm.rope_dim)))
    w_uq = jnp.concatenate([uq[:, :, :LANES].reshape(dm.q_rank, h * LANES),
                            uq_rope.reshape(dm.q_rank, h * LANES)], axis=1).astype(BF16)
    ukv = w_mla_ukv[l].reshape(dm.kv_rank, h, 2 * LANES)
    w_ukv = jnp.concatenate([ukv[:, :, :LANES].reshape(dm.kv_rank, h * LANES),
                             ukv[:, :, LANES:].reshape(dm.kv_rank, h * LANES)], axis=1).astype(BF16)
    cast = lambda w: w[l].astype(BF16)
    return dict(ff1_in=cast(w_ff1_in), ff1_out=cast(w_ff1_out), ff2_in=cast(w_ff2_in), ff2_out=cast(w_ff2_out),
                proj=w_proj, uq=w_uq, ukv=w_ukv, o_gqa=cast(w_o_gqa), o_mla=cast(w_o_mla),
                o_lru=cast(w_o_lru), out=cast(w_out))


def _forward(dm, x, c, ctx, c_ctx, w_mod, b_mod, g_norm, w_ff1_in, w_ff1_out, w_ff2_in, w_ff2_out, w_in,
             gqa_q_norm, gqa_k_norm, mla_q_norm, mla_kv_norm, w_mla_uq, w_mla_ukv, lru_conv_w, lru_conv_b,
             lru_wa, lru_ba, lru_wi, lru_bi, lru_lambda, w_o_gqa, w_o_mla, w_o_lru, w_out, final_norm):
    b, tl, d = x.shape
    tc = ctx.shape[1]
    tt = tl + tc
    depth = w_mod.shape[0]
    assert b == 2 and tl % ROW_TILE == 0 and tc % ROW_TILE == 0
    lay = _proj_layout(dm, d)
    tabs_g = _rope_tables(tl, LANES, dm.grid_w, tc)
    tabs_m = _rope_tables(tl, dm.rope_dim, dm.grid_w, tc)

    c_rows = jnp.concatenate([c, c_ctx[None, :], jnp.zeros((SUBLANES - b - 1, d), F32)], axis=0)
    mods_all = _mod_vectors(c_rows, w_mod, b_mod)

    xc = ctx
    for l in range(depth):
        last = l == depth - 1
        w = _layer_weights(l, dm, lay, w_ff1_in, w_ff1_out, w_ff2_in, w_ff2_out, w_in, w_mla_uq, w_mla_ukv,
                           w_o_gqa, w_o_mla, w_o_lru, w_out)
        mods = mods_all[l, :b + 1].reshape((b + 1) * N_MOD, 1, d)

        h1 = _adaln_norm(x, xc, g_norm[l, 0], mods, 0, 1, True)
        act = _swiglu_in(h1.reshape(b * tt, d), w['ff1_in']).reshape(b, tt, dm.d_ff)
        x = _residual_matmul(act, w['ff1_out'], x, mods, 2, 0.5, 0, False)
        xc = _residual_matmul(act, w['ff1_out'], xc, mods, 2, 0.5, tl, True)

        h2 = _adaln_norm(x, xc, g_norm[l, 1], mods, 3, 4, True)
        proj = _matmul(h2.reshape(b * tt, d), w['proj']).reshape(b, tt, lay.total)
        k_gqa, cq_n, ckv_n, k_rope = _prep(proj, lay, dm, tabs_g, tabs_m, gqa_k_norm[l], mla_q_norm[l],
                                           mla_kv_norm[l])
        q_up = _matmul(cq_n.reshape(b * tt, dm.q_rank), w['uq']).reshape(b, tt, -1)
        kv_up = _matmul(ckv_n.reshape(b * tt, dm.kv_rank), w['ukv']).reshape(b, tt, -1)
        y_lru = _lru(proj, lay, dm, tl, lru_conv_w[l], lru_conv_b[l], lru_wa[l], lru_ba[l], lru_wi[l],
                     lru_bi[l], lru_lambda[l])

        o_gqa = _gqa_attention(proj, k_gqa, lay, dm, tabs_g, gqa_q_norm[l], tl, 0, tt, 0)
        o_mla = _mla_attention(q_up, kv_up, k_rope, dm, tabs_m, tl, 0, tt, 0)
        merged = _merge(o_gqa, o_mla, y_lru, proj, w['o_gqa'], w['o_mla'], w['o_lru'], lay.gate, tl, 0)
        x = _residual_matmul(merged, w['out'], x, mods, 5, None, 0, False)
        if not last:
            o_gqa_c = _gqa_attention(proj, k_gqa, lay, dm, tabs_g, gqa_q_norm[l], tc, tl, tc, tl)
            o_mla_c = _mla_attention(q_up, kv_up, k_rope, dm, tabs_m, tc, tl, tc, tl)
            merged_c = _merge(o_gqa_c, o_mla_c, y_lru, proj, w['o_gqa'], w['o_mla'], w['o_lru'],
                              lay.gate, tc, tl)
            xc = _residual_matmul(merged_c, w['out'], xc, mods, 5, None, 0, True)

        h3 = _adaln_norm(x, xc, g_norm[l, 2], mods, 6, 7, not last)
        rows = h3.shape[1]
        act = _swiglu_in(h3.reshape(b * rows, d), w['ff2_in']).reshape(b, rows, dm.d_ff)
        x = _residual_matmul(act, w['ff2_out'], x, mods, 8, 0.5, 0, False)
        if not last:
            xc = _residual_matmul(act, w['ff2_out'], xc, mods, 8, 0.5, tl, True)
    return _final_norm(x, final_norm)


def kernel(x, c, ctx, c_ctx, w_mod, b_mod, g_norm, w_ff1_in, w_ff1_out, w_ff2_in, w_ff2_out, w_in, gqa_q_norm, gqa_k_norm, mla_q_norm, mla_kv_norm, w_mla_uq, w_mla_ukv, lru_conv_w, lru_conv_b, lru_wa, lru_ba, lru_wi, lru_bi, lru_lambda, w_o_gqa, w_o_mla, w_o_lru, w_out, final_norm):
    return _forward(DIMS, x, c, ctx, c_ctx, w_mod, b_mod, g_norm, w_ff1_in, w_ff1_out, w_ff2_in, w_ff2_out,
                    w_in, gqa_q_norm, gqa_k_norm, mla_q_norm, mla_kv_norm, w_mla_uq, w_mla_ukv, lru_conv_w,
                    lru_conv_b, lru_wa, lru_ba, lru_wi, lru_bi, lru_lambda, w_o_gqa, w_o_mla, w_o_lru, w_out,
                    final_norm)
```

```python
import dataclasses
import functools

import jax
import jax.numpy as jnp
from jax import lax
from jax.experimental import pallas as pl
from jax.experimental.pallas import tpu as pltpu

F32 = jnp.float32
BF16 = jnp.bfloat16

LANES = 128
SUBLANES = 8
VMEM_LIMIT = 60 * 1024 * 1024

ROPE_THETA = 10000.0
NORM_EPS = 1e-6
N_MOD = 9
LRU_C = 8.0
CONV_WIDTH = 4
ROW_TILE = 256


@dataclasses.dataclass(frozen=True)
class Dims:
    grid_w: int = 64
    gqa_heads: int = 16
    gqa_kv: int = 4
    mla_heads: int = 16
    q_rank: int = 1536
    kv_rank: int = 512
    rope_dim: int = 64
    lru_w: int = 2048
    d_ff: int = 6144
    proj_pad: int = 1024


DIMS = Dims()


def _params(n_grid):
    return pltpu.CompilerParams(dimension_semantics=("arbitrary",) * n_grid,
                                vmem_limit_bytes=VMEM_LIMIT)


def _tile(dim, pref, align):
    best = None
    t = align
    while t <= min(dim, pref):
        if dim % t == 0:
            best = t
        t += align
    return best if best is not None else dim


@dataclasses.dataclass(frozen=True)
class ProjLayout:
    q: int
    k: int
    v: int
    cq: int
    ckv: int
    krope: int
    lx: int
    lg: int
    gate: int
    total: int


def _proj_layout(dm, d_model):
    q = 0
    k = q + dm.gqa_heads * LANES
    v = k + dm.gqa_kv * LANES
    cq = v + dm.gqa_kv * LANES
    ckv = cq + dm.q_rank
    krope = ckv + dm.kv_rank
    attn_end = krope + LANES
    lx = -(-attn_end // dm.proj_pad) * dm.proj_pad
    lg = lx + dm.lru_w
    gate = lg + dm.lru_w
    total = gate + 3 * d_model
    return ProjLayout(q, k, v, cq, ckv, krope, lx, lg, gate, total)


def _mod_kernel(c_ref, w_ref, b_ref, o_ref):
    c = c_ref[...]
    a = (c * jax.nn.sigmoid(c)).astype(BF16)
    acc = jnp.dot(a, w_ref[...].astype(BF16), preferred_element_type=F32)
    o_ref[...] = acc + b_ref[...]


def _mod_vectors(c_rows, w_mod, b_mod):
    n_layers, d, n = w_mod.shape
    tn = _tile(n, 512, LANES)
    return pl.pallas_call(
        _mod_kernel,
        grid=(n_layers, n // tn),
        in_specs=[
            pl.BlockSpec((SUBLANES, d), lambda l, j: (0, 0)),
            pl.BlockSpec((None, d, tn), lambda l, j: (l, 0, j)),
            pl.BlockSpec((None, 1, tn), lambda l, j: (l, 0, j)),
        ],
        out_specs=pl.BlockSpec((None, SUBLANES, tn), lambda l, j: (l, 0, j)),
        out_shape=jax.ShapeDtypeStruct((n_layers, SUBLANES, n), F32),
        compiler_params=_params(2),
        name="mod_vectors",
    )(c_rows, w_mod, b_mod.reshape(n_layers, 1, n))


def _rms(x, g):
    return x * lax.rsqrt(jnp.mean(x * x, axis=-1, keepdims=True) + NORM_EPS) * g


def _adaln_kernel(xl_ref, xc_ref, g_ref, shift_ref, scale_ref, o_ref, *, n_lat):
    def emit(x_ref):
        y = _rms(x_ref[...], g_ref[...])
        o_ref[...] = (y * (1.0 + scale_ref[0]) + shift_ref[0]).astype(o_ref.dtype)

    t = pl.program_id(1)
    pl.when(t < n_lat)(lambda: emit(xl_ref))
    pl.when(t >= n_lat)(lambda: emit(xc_ref))


def _adaln_norm(x_lat, x_ctx, g, mods, idx_shift, idx_scale, with_ctx):
    b, tl, d = x_lat.shape
    tc = x_ctx.shape[1]
    n_lat = tl // ROW_TILE
    n_ctx = tc // ROW_TILE if with_ctx else 0
    n_rows = tl + (tc if with_ctx else 0)

    def mod_row(bi, t):
        return jnp.where(t < n_lat, bi, 2) * N_MOD

    return pl.pallas_call(
        functools.partial(_adaln_kernel, n_lat=n_lat),
        grid=(b, n_lat + n_ctx),
        in_specs=[
            pl.BlockSpec((None, ROW_TILE, d), lambda bi, t: (bi, jnp.minimum(t, n_lat - 1), 0)),
            pl.BlockSpec((None, ROW_TILE, d), lambda bi, t: (bi, jnp.maximum(t - n_lat, 0), 0)),
            pl.BlockSpec((1, d), lambda bi, t: (0, 0)),
            pl.BlockSpec((1, 1, d), lambda bi, t: (mod_row(bi, t) + idx_shift, 0, 0)),
            pl.BlockSpec((1, 1, d), lambda bi, t: (mod_row(bi, t) + idx_scale, 0, 0)),
        ],
        out_specs=pl.BlockSpec((None, ROW_TILE, d), lambda bi, t: (bi, t, 0)),
        out_shape=jax.ShapeDtypeStruct((b, n_rows, d), BF16),
        compiler_params=_params(2),
        name="adaln_norm",
    )(x_lat, x_ctx, g.reshape(1, d), mods, mods)


def _final_norm_kernel(x_ref, g_ref, o_ref):
    o_ref[...] = _rms(x_ref[...], g_ref[...])


def _final_norm(x, g):
    b, t, d = x.shape
    return pl.pallas_call(
        _final_norm_kernel,
        grid=(b, t // ROW_TILE),
        in_specs=[pl.BlockSpec((None, ROW_TILE, d), lambda bi, i: (bi, i, 0)),
                  pl.BlockSpec((1, d), lambda bi, i: (0, 0))],
        out_specs=pl.BlockSpec((None, ROW_TILE, d), lambda bi, i: (bi, i, 0)),
        out_shape=jax.ShapeDtypeStruct((b, t, d), F32),
        compiler_params=_params(2),
        name="final_norm",
    )(x, g.reshape(1, d))


def _mm_kernel(a_ref, b_ref, o_ref):
    o_ref[...] = jnp.dot(a_ref[...], b_ref[...], preferred_element_type=F32).astype(o_ref.dtype)


def _matmul(a, w, tm_pref=1536, tn_pref=512):
    m, k = a.shape
    n = w.shape[1]
    tm = _tile(m, tm_pref, ROW_TILE)
    tn = _tile(n, tn_pref, LANES)
    return pl.pallas_call(
        _mm_kernel,
        grid=(m // tm, n // tn),
        in_specs=[pl.BlockSpec((tm, k), lambda i, j: (i, 0)),
                  pl.BlockSpec((k, tn), lambda i, j: (0, j))],
        out_specs=pl.BlockSpec((tm, tn), lambda i, j: (i, j)),
        out_shape=jax.ShapeDtypeStruct((m, n), BF16),
        compiler_params=_params(2),
        name="matmul",
    )(a, w)


def _swiglu_kernel(a_ref, wg_ref, wu_ref, o_ref):
    a = a_ref[...]
    g = jnp.dot(a, wg_ref[...], preferred_element_type=F32)
    u = jnp.dot(a, wu_ref[...], preferred_element_type=F32)
    o_ref[...] = (g * jax.nn.sigmoid(g) * u).astype(o_ref.dtype)


def _swiglu_in(h, w_in, tm_pref=1536, tn_pref=512):
    m, k = h.shape
    f = w_in.shape[1] // 2
    tm = _tile(m, tm_pref, ROW_TILE)
    tn = _tile(f, tn_pref, LANES)
    nj = f // tn
    return pl.pallas_call(
        _swiglu_kernel,
        grid=(m // tm, nj),
        in_specs=[pl.BlockSpec((tm, k), lambda i, j: (i, 0)),
                  pl.BlockSpec((k, tn), lambda i, j: (0, j)),
                  pl.BlockSpec((k, tn), lambda i, j: (0, j + nj))],
        out_specs=pl.BlockSpec((tm, tn), lambda i, j: (i, j)),
        out_shape=jax.ShapeDtypeStruct((m, f), BF16),
        compiler_params=_params(2),
        name="swiglu_in",
    )(h, w_in, w_in)


def _residual_kernel(a_ref, w_ref, x_ref, gate_ref, o_ref, *, coef):
    y = jnp.dot(a_ref[...], w_ref[...], preferred_element_type=F32)
    gate = gate_ref[0] if coef is None else coef * gate_ref[0]
    o_ref[...] = x_ref[...] + gate * y


def _residual_matmul(a, w, x, mods, idx_gate, coef, a_row_off, is_ctx, tm_pref=1024, tn_pref=512):
    b, r, n = x.shape
    k = a.shape[2]
    tm = _tile(r, tm_pref, ROW_TILE)
    tn = _tile(n, tn_pref, LANES)
    off = a_row_off // tm

    def gate_row(bi):
        return (2 if is_ctx else bi) * N_MOD + idx_gate

    return pl.pallas_call(
        functools.partial(_residual_kernel, coef=coef),
        grid=(b, r // tm, n // tn),
        in_specs=[pl.BlockSpec((None, tm, k), lambda bi, i, j: (bi, i + off, 0)),
                  pl.BlockSpec((k, tn), lambda bi, i, j: (0, j)),
                  pl.BlockSpec((None, tm, tn), lambda bi, i, j: (bi, i, j)),
                  pl.BlockSpec((1, 1, tn), lambda bi, i, j: (gate_row(bi), 0, j))],
        out_specs=pl.BlockSpec((None, tm, tn), lambda bi, i, j: (bi, i, j)),
        out_shape=jax.ShapeDtypeStruct((b, r, n), F32),
        compiler_params=_params(3),
        name="residual_matmul",
    )(a, w, x, mods)


def _merge_kernel(a0_ref, a1_ref, a2_ref, w0_ref, w1_ref, w2_ref, g0_ref, g1_ref, g2_ref, o_ref):
    def branch(a_ref, w_ref, g_ref):
        y = jnp.dot(a_ref[...], w_ref[...], preferred_element_type=F32)
        return jax.nn.sigmoid(g_ref[...].astype(F32)) * y

    acc = branch(a0_ref, w0_ref, g0_ref) + branch(a1_ref, w1_ref, g1_ref)
    o_ref[...] = (acc + branch(a2_ref, w2_ref, g2_ref)).astype(o_ref.dtype)


def _merge(o_gqa, o_mla, y_lru, proj, w0, w1, w2, gate_col, rows, mix_row_off, tm_pref=1024, tn_pref=512):
    b = o_gqa.shape[0]
    n = w0.shape[1]
    tm = _tile(rows, tm_pref, ROW_TILE)
    tn = _tile(n, tn_pref, LANES)
    off = mix_row_off // tm
    gcol = gate_col // tn
    nj = n // tn

    def a_spec(a, row_off):
        return pl.BlockSpec((None, tm, a.shape[2]), lambda bi, i, j: (bi, i + row_off, 0))

    def w_spec(w):
        return pl.BlockSpec((w.shape[0], tn), lambda bi, i, j: (0, j))

    def gate_spec(br):
        return pl.BlockSpec((None, tm, tn), lambda bi, i, j: (bi, i + off, gcol + br * nj + j))

    return pl.pallas_call(
        _merge_kernel,
        grid=(b, rows // tm, nj),
        in_specs=[a_spec(o_gqa, 0), a_spec(o_mla, 0), a_spec(y_lru, off), w_spec(w0), w_spec(w1), w_spec(w2),
                  gate_spec(0), gate_spec(1), gate_spec(2)],
        out_specs=pl.BlockSpec((None, tm, tn), lambda bi, i, j: (bi, i, j)),
        out_shape=jax.ShapeDtypeStruct((b, rows, n), BF16),
        compiler_params=_params(3),
        name="merge",
    )(o_gqa, o_mla, y_lru, w0, w1, w2, proj, proj, proj)


def _rope_tables(n_tok, rot_dim, grid_w, ctx_len):
    rows = n_tok // grid_w
    r_idx, c_idx = jnp.meshgrid(jnp.arange(rows), jnp.arange(grid_w), indexing='ij')
    r_idx = r_idx.reshape(-1).astype(F32)
    c_idx = c_idx.reshape(-1).astype(F32)
    n_pairs = rot_dim // 4
    freqs = jnp.power(ROPE_THETA, -jnp.arange(n_pairs, dtype=F32) / n_pairs)
    ang = jnp.concatenate([r_idx[:, None] * freqs, c_idx[:, None] * freqs], axis=-1)
    cos, sin = jnp.cos(ang), jnp.sin(ang)
    cos_full = jnp.concatenate([cos, cos], axis=-1)
    sin_signed = jnp.concatenate([-sin, sin], axis=-1)
    reps = LANES // rot_dim
    cos_full = jnp.tile(cos_full, (1, reps))
    sin_signed = jnp.tile(sin_signed, (1, reps))
    cos_full = jnp.concatenate([cos_full, jnp.ones((ctx_len, LANES), F32)], axis=0)
    sin_signed = jnp.concatenate([sin_signed, jnp.zeros((ctx_len, LANES), F32)], axis=0)
    return cos_full, sin_signed


def _rope128(x, cos, sin):
    return x * cos + pltpu.roll(x, LANES // 2, 1) * sin


def _rope64(x, cos, sin):
    lane = lax.broadcasted_iota(jnp.int32, x.shape, 1)
    first_half = (lane % 64) < 32
    partner = jnp.where(first_half, pltpu.roll(x, LANES - 32, 1), pltpu.roll(x, 32, 1))
    return x * cos + partner * sin


def _prep_kernel(k_ref, cq_ref, ckv_ref, kr_ref, cg_ref, sg_ref, cm_ref, sm_ref,
                 gk_ref, gq_ref, gkv_ref, ko_ref, cqo_ref, ckvo_ref, kro_ref, *, kv_heads):
    cos_g, sin_g = cg_ref[...], sg_ref[...]
    for h in range(kv_heads):
        cols = slice(h * LANES, (h + 1) * LANES)
        y = _rms(k_ref[:, cols].astype(F32), gk_ref[...])
        ko_ref[:, cols] = _rope128(y, cos_g, sin_g).astype(BF16)
    cqo_ref[...] = _rms(cq_ref[...].astype(F32), gq_ref[...]).astype(BF16)
    ckvo_ref[...] = _rms(ckv_ref[...].astype(F32), gkv_ref[...]).astype(BF16)
    kro_ref[...] = _rope64(kr_ref[...].astype(F32), cm_ref[...], sm_ref[...]).astype(BF16)


def _prep(proj, lay, dm, tabs_g, tabs_m, gk, gq, gkv):
    b, tt, _ = proj.shape
    kw = dm.gqa_kv * LANES
    assert lay.k % kw == 0 and lay.cq % dm.q_rank == 0 and lay.ckv % dm.kv_rank == 0
    row = lambda bi, t: (bi, t, 0)
    tab = pl.BlockSpec((ROW_TILE, LANES), lambda bi, t: (t, 0))

    def vec(n):
        return pl.BlockSpec((1, n), lambda bi, t: (0, 0))

    return pl.pallas_call(
        functools.partial(_prep_kernel, kv_heads=dm.gqa_kv),
        grid=(b, tt // ROW_TILE),
        in_specs=[pl.BlockSpec((None, ROW_TILE, kw), lambda bi, t: (bi, t, lay.k // kw)),
                  pl.BlockSpec((None, ROW_TILE, dm.q_rank), lambda bi, t: (bi, t, lay.cq // dm.q_rank)),
                  pl.BlockSpec((None, ROW_TILE, dm.kv_rank), lambda bi, t: (bi, t, lay.ckv // dm.kv_rank)),
                  pl.BlockSpec((None, ROW_TILE, LANES), lambda bi, t: (bi, t, lay.krope // LANES)),
                  tab, tab, tab, tab, vec(LANES), vec(dm.q_rank), vec(dm.kv_rank)],
        out_specs=[pl.BlockSpec((None, ROW_TILE, kw), row),
                   pl.BlockSpec((None, ROW_TILE, dm.q_rank), row),
                   pl.BlockSpec((None, ROW_TILE, dm.kv_rank), row),
                   pl.BlockSpec((None, ROW_TILE, LANES), row)],
        out_shape=[jax.ShapeDtypeStruct((b, tt, kw), BF16),
                   jax.ShapeDtypeStruct((b, tt, dm.q_rank), BF16),
                   jax.ShapeDtypeStruct((b, tt, dm.kv_rank), BF16),
                   jax.ShapeDtypeStruct((b, tt, LANES), BF16)],
        compiler_params=_params(2),
        name="prep",
    )(proj, proj, proj, proj, tabs_g[0], tabs_g[1], tabs_m[0], tabs_m[1],
      gk.reshape(1, LANES), gq.reshape(1, dm.q_rank), gkv.reshape(1, dm.kv_rank))


LOG2E = 1.4426950408889634
KEY_CHUNK = 768


SCORE_BUFFERS = 4
ONES_ROWS = 16


def _flash_scratch(dk, n_q, key_rows, tk):
    return ([pltpu.VMEM((dk, n_q), BF16),
             pltpu.VMEM((key_rows // tk, LANES + ONES_ROWS, tk), BF16),
             pltpu.VMEM((1, n_q), F32),
             pltpu.VMEM((LANES + ONES_ROWS, n_q), F32)]
            + [pltpu.VMEM((tk, n_q), F32)] * SCORE_BUFFERS
            + [pltpu.VMEM((1, n_q), F32)] * SCORE_BUFFERS)


def _flash_t(qt_ref, load_k, load_vt, n_chunks, stats):
    m_ref, acc_ref = stats[:2]
    s_refs = stats[2:2 + SCORE_BUFFERS]
    smax_refs = stats[2 + SCORE_BUFFERS:]
    m_ref[...] = jnp.full(m_ref.shape, -jnp.inf, F32)
    acc_ref[...] = jnp.zeros(acc_ref.shape, F32)

    def scores(c, slot):
        s = jnp.dot(load_k(c), qt_ref[...], preferred_element_type=F32)
        s_refs[slot][...] = s
        smax_refs[slot][...] = jnp.max(s, axis=0, keepdims=True)

    def update(c, slot):
        m_prev = m_ref[...]
        m_new = jnp.maximum(m_prev, smax_refs[slot][...])
        p = jnp.exp2(s_refs[slot][...] - m_new)
        alpha = jnp.exp2(m_prev - m_new)
        acc_ref[...] = alpha * acc_ref[...] + jnp.dot(load_vt(c), p.astype(BF16),
                                                     preferred_element_type=F32)
        m_ref[...] = m_new

    scores(0, 0)
    n_main = (n_chunks - 1) // SCORE_BUFFERS

    def body(i, carry):
        c = SCORE_BUFFERS * i
        for j in range(SCORE_BUFFERS):
            scores(c + j + 1, (j + 1) % SCORE_BUFFERS)
            update(c + j, j)
        return carry

    lax.fori_loop(0, n_main, body, 0)
    for c in range(SCORE_BUFFERS * n_main, n_chunks):
        if c + 1 < n_chunks:
            scores(c + 1, (c + 1) % SCORE_BUFFERS)
        update(c, c % SCORE_BUFFERS)
    return acc_ref[:LANES, :] / acc_ref[LANES:LANES + 1, :]


def _stage_values_t(v_ref, vt_ref, tk):
    @pl.when(pl.program_id(2) == 0)
    def _():
        for c in range(vt_ref.shape[0]):
            vt_ref[c, :LANES, :] = v_ref[c * tk:(c + 1) * tk, :].astype(F32).T.astype(BF16)
            vt_ref[c, LANES:, :] = jnp.ones((ONES_ROWS, tk), BF16)


def _gqa_kernel(q_ref, k_ref, v_ref, cos_ref, sin_ref, gq_ref, o_ref, qt_ref, vt_ref, *stats,
                group, tq, tk, scale):
    _stage_values_t(v_ref, vt_ref, tk)
    cos, sin = cos_ref[...], sin_ref[...]
    for g in range(group):
        y = _rms(q_ref[:, g * LANES:(g + 1) * LANES].astype(F32), gq_ref[...])
        qt_ref[:, g * tq:(g + 1) * tq] = (_rope128(y, cos, sin) * scale).T.astype(BF16)
    out_t = _flash_t(qt_ref,
                     lambda c: k_ref[pl.ds(pl.multiple_of(c * tk, tk), tk), :],
                     lambda c: vt_ref[c],
                     vt_ref.shape[0], stats)
    for g in range(group):
        o_ref[:, g * LANES:(g + 1) * LANES] = out_t[:, g * tq:(g + 1) * tq].T.astype(o_ref.dtype)


def _gqa_attention(proj, k_prep, lay, dm, tabs, gq, q_rows, q_off, key_rows, key_off, tq_pref=256):
    b = proj.shape[0]
    group = dm.gqa_heads // dm.gqa_kv
    gw = group * LANES
    tq = _tile(q_rows, tq_pref, ROW_TILE)
    tk = _tile(key_rows, KEY_CHUNK, ROW_TILE)
    assert q_off % tq == 0 and key_off % key_rows == 0
    qo, ko, vcol = q_off // tq, key_off // key_rows, lay.v // LANES
    return pl.pallas_call(
        functools.partial(_gqa_kernel, group=group, tq=tq, tk=tk, scale=float(LANES) ** -0.5 * LOG2E),
        grid=(b, dm.gqa_kv, q_rows // tq),
        in_specs=[pl.BlockSpec((None, tq, gw), lambda bi, h, i: (bi, i + qo, h)),
                  pl.BlockSpec((None, key_rows, LANES), lambda bi, h, i: (bi, ko, h)),
                  pl.BlockSpec((None, key_rows, LANES), lambda bi, h, i: (bi, ko, vcol + h)),
                  pl.BlockSpec((tq, LANES), lambda bi, h, i: (i + qo, 0)),
                  pl.BlockSpec((tq, LANES), lambda bi, h, i: (i + qo, 0)),
                  pl.BlockSpec((1, LANES), lambda bi, h, i: (0, 0))],
        out_specs=pl.BlockSpec((None, tq, gw), lambda bi, h, i: (bi, i, h)),
        out_shape=jax.ShapeDtypeStruct((b, q_rows, dm.gqa_heads * LANES), BF16),
        scratch_shapes=_flash_scratch(LANES, group * tq, key_rows, tk),
        compiler_params=_params(3),
        name="gqa_attention",
    )(proj, k_prep, proj, tabs[0], tabs[1], gq.reshape(1, LANES))


def _mla_kernel(qn_ref, qr_ref, kn_ref, kr_ref, v_ref, cos_ref, sin_ref, o_ref,
                qt_ref, vt_ref, *stats, tk, scale):
    _stage_values_t(v_ref, vt_ref, tk)
    qt_ref[:LANES, :] = (qn_ref[...].astype(F32) * scale).T.astype(BF16)
    qr = _rope64(qr_ref[...].astype(F32), cos_ref[...], sin_ref[...])
    qt_ref[LANES:, :] = (qr * scale).T.astype(BF16)

    def load_k(c):
        rows = pl.ds(pl.multiple_of(c * tk, tk), tk)
        return jnp.concatenate([kn_ref[rows, :], kr_ref[rows, :]], axis=1)

    out_t = _flash_t(qt_ref, load_k, lambda c: vt_ref[c], vt_ref.shape[0], stats)
    o_ref[...] = out_t.T.astype(o_ref.dtype)


def _mla_attention(q_up, kv_up, k_rope, dm, tabs, q_rows, q_off, key_rows, key_off, tq_pref=1024):
    b = q_up.shape[0]
    h_n = dm.mla_heads
    tq = _tile(q_rows, tq_pref, ROW_TILE)
    tk = _tile(key_rows, KEY_CHUNK, ROW_TILE)
    assert q_off % tq == 0 and key_off % key_rows == 0
    qo, ko = q_off // tq, key_off // key_rows
    scale = float(LANES + dm.rope_dim) ** -0.5 * LOG2E
    return pl.pallas_call(
        functools.partial(_mla_kernel, tk=tk, scale=scale),
        grid=(b, h_n, q_rows // tq),
        in_specs=[pl.BlockSpec((None, tq, LANES), lambda bi, h, i: (bi, i + qo, h)),
                  pl.BlockSpec((None, tq, LANES), lambda bi, h, i: (bi, i + qo, h_n + h)),
                  pl.BlockSpec((None, key_rows, LANES), lambda bi, h, i: (bi, ko, h)),
                  pl.BlockSpec((None, key_rows, LANES), lambda bi, h, i: (bi, ko, 0)),
                  pl.BlockSpec((None, key_rows, LANES), lambda bi, h, i: (bi, ko, h_n + h)),
                  pl.BlockSpec((tq, LANES), lambda bi, h, i: (i + qo, 0)),
                  pl.BlockSpec((tq, LANES), lambda bi, h, i: (i + qo, 0))],
        out_specs=pl.BlockSpec((None, tq, LANES), lambda bi, h, i: (bi, i, h)),
        out_shape=jax.ShapeDtypeStruct((b, q_rows, h_n * LANES), BF16),
        scratch_shapes=_flash_scratch(2 * LANES, tq, key_rows, tk),
        compiler_params=_params(3),
        name="mla_attention",
    )(q_up, q_up, kv_up, k_rope, kv_up, tabs[0], tabs[1])


def _lru_kernel(lx_ref, lg_ref, cw_ref, cb_ref, wa_ref, ba_ref, wi_ref, bi_ref, lam_ref, o_ref,
                xs_ref, xc_ref, hf_ref, hb_ref, *, tl, tc, ch):
    tt = tl + tc
    n_lat, n_ctx = tl // ch, tc // ch
    pad = SUBLANES
    zeros = jnp.zeros((pad, LANES), F32)
    xs_ref[0:pad, :] = zeros
    xs_ref[pad + tt:2 * pad + tt, :] = zeros

    def stage(c, carry):
        r = pl.multiple_of(c * ch, ch)
        xs_ref[pl.ds(pad + r, ch), :] = lx_ref[pl.ds(r, ch), :].astype(F32)
        return carry

    lax.fori_loop(0, n_lat + n_ctx, stage, 0)

    def conv(c, carry):
        r = pl.multiple_of(c * ch, ch)
        ext = xs_ref[pl.ds(r, ch + 2 * pad), :]
        n_ext = ch + 2 * pad
        starts = jnp.logical_or(c == 0, c == n_lat)
        ends = jnp.logical_or(c == n_lat - 1, c == n_lat + n_ctx - 1)
        head = jnp.where(starts, 0.0, ext[:pad])
        tail = jnp.where(ends, 0.0, ext[pad + ch:])
        ext = jnp.concatenate([head, ext[pad:pad + ch], tail], axis=0)
        y = cb_ref[...]
        for j in range(CONV_WIDTH):
            shift = j - 1
            tap = ext if shift == 0 else pltpu.roll(ext, (-shift) % n_ext, 0)
            y = y + tap[pad:pad + ch] * cw_ref[j:j + 1, :]
        xc_ref[pl.ds(r, ch), :] = y
        return carry

    lax.fori_loop(0, n_lat + n_ctx, conv, 0)

    row = lax.broadcasted_iota(jnp.int32, (SUBLANES, LANES), 0)

    def coeffs(xc, d):
        xb = xc.astype(BF16)
        r = jax.nn.sigmoid(jnp.dot(xb, wa_ref[d].astype(BF16), preferred_element_type=F32) + ba_ref[d:d + 1, :])
        i = jax.nn.sigmoid(jnp.dot(xb, wi_ref[d].astype(BF16), preferred_element_type=F32) + bi_ref[d:d + 1, :])
        z = -lam_ref[d:d + 1, :]
        softplus = jnp.maximum(z, 0.0) + jnp.log1p(jnp.exp(-jnp.abs(z)))
        log_a = -LRU_C * r * softplus
        a = jnp.exp(log_a)
        u = jnp.sqrt(1.0 - a * a) * (i * xc)
        return a, u

    def scan_chunk(c, h, d, out_ref):
        r = pl.multiple_of(c * ch, ch)
        a, u = coeffs(xc_ref[pl.ds(r, ch), :], d)
        n_blk = ch // SUBLANES
        order = range(n_blk) if d == 0 else range(n_blk - 1, -1, -1)
        edge = SUBLANES - 1 if d == 0 else 0
        blocks = []
        for j in range(n_blk):
            aj = a[j * SUBLANES:(j + 1) * SUBLANES]
            uj = u[j * SUBLANES:(j + 1) * SUBLANES]
            for dist in (1, 2, 4):
                if d == 0:
                    shift, mask = dist, row >= dist
                else:
                    shift, mask = SUBLANES - dist, row < SUBLANES - dist
                a_s = jnp.where(mask, pltpu.roll(aj, shift, 0), 1.0)
                u_s = jnp.where(mask, pltpu.roll(uj, shift, 0), 0.0)
                uj = uj + aj * u_s
                aj = aj * a_s
            blocks.append((aj, uj))
        carry_in = [None] * n_blk
        for j in order:
            carry_in[j] = h
            aj, uj = blocks[j]
            h = aj[edge:edge + 1] * h + uj[edge:edge + 1]
        states = [blocks[j][0] * carry_in[j] + blocks[j][1] for j in range(n_blk)]
        out_ref[pl.ds(r, ch), :] = jnp.concatenate(states, axis=0)
        return h

    def both(fwd_chunk, bwd_chunk, carry):
        return (scan_chunk(fwd_chunk, carry[0], 0, hf_ref), scan_chunk(bwd_chunk, carry[1], 1, hb_ref))

    h0 = jnp.zeros((1, LANES), F32)
    carry = lax.fori_loop(0, n_ctx, lambda k, cr: both(n_lat + k, n_lat + n_ctx - 1 - k, cr), (h0, h0))
    lax.fori_loop(0, n_lat, lambda k, cr: both(k, n_lat - 1 - k, cr), carry)

    def combine(c, carry):
        rows = pl.ds(pl.multiple_of(c * ch, ch), ch)
        gate = jax.nn.gelu(lg_ref[rows, :].astype(F32))
        o_ref[rows, :] = ((hf_ref[rows, :] + hb_ref[rows, :]) * gate).astype(o_ref.dtype)
        return carry

    lax.fori_loop(0, n_lat + n_ctx, combine, 0)


def _lru(proj, lay, dm, tl, conv_w, conv_b, wa, ba, wi, bi, lam):
    b, tt, _ = proj.shape
    nb = dm.lru_w // LANES
    xcol, gcol = lay.lx // LANES, lay.lg // LANES
    vec2 = pl.BlockSpec((2, LANES), lambda bi_, n: (0, n))
    wspec = pl.BlockSpec((2, None, LANES, LANES), lambda bi_, n: (0, n, 0, 0))
    return pl.pallas_call(
        functools.partial(_lru_kernel, tl=tl, tc=tt - tl, ch=ROW_TILE),
        grid=(b, nb),
        in_specs=[pl.BlockSpec((None, tt, LANES), lambda bi_, n: (bi_, 0, xcol + n)),
                  pl.BlockSpec((None, tt, LANES), lambda bi_, n: (bi_, 0, gcol + n)),
                  pl.BlockSpec((CONV_WIDTH, LANES), lambda bi_, n: (0, n)),
                  pl.BlockSpec((1, LANES), lambda bi_, n: (0, n)),
                  wspec, vec2, wspec, vec2, vec2],
        out_specs=pl.BlockSpec((None, tt, LANES), lambda bi_, n: (bi_, 0, n)),
        out_shape=jax.ShapeDtypeStruct((b, tt, dm.lru_w), BF16),
        scratch_shapes=[pltpu.VMEM((tt + 2 * SUBLANES, LANES), F32),
                        pltpu.VMEM((tt, LANES), F32),
                        pltpu.VMEM((tt, LANES), F32),
                        pltpu.VMEM((tt, LANES), F32)],
        compiler_params=_params(2),
        name="rglru",
    )(proj, proj, conv_w, conv_b.reshape(1, dm.lru_w), wa, ba, wi, bi, lam)


def _cast_kernel(w_ref, o_ref):
    o_ref[...] = w_ref[...].astype(o_ref.dtype)


def _cast_layer(w, l):
    _, r, c = w.shape
    tr = _tile(r, 512, SUBLANES)
    tc = _tile(c, 2048, LANES)
    return pl.pallas_call(
        _cast_kernel,
        grid=(r // tr, c // tc),
        in_specs=[pl.BlockSpec((None, tr, tc), lambda i, j: (l, i, j))],
        out_specs=pl.BlockSpec((tr, tc), lambda i, j: (i, j)),
        out_shape=jax.ShapeDtypeStruct((r, c), BF16),
        compiler_params=_params(2),
        name="cast_weight",
    )(w)


def _proj_weight_kernel(a_ref, b_ref, o_ref, *, n_copy, n_lx, shift):
    j = pl.program_id(1)

    @pl.when(j < n_copy)
    def _():
        o_ref[...] = b_ref[...].astype(o_ref.dtype)

    @pl.when(j == n_copy)
    def _():
        lane = lax.broadcasted_iota(jnp.int32, b_ref.shape, 1)
        o_ref[...] = jnp.where(lane < shift, b_ref[...], 0.0).astype(o_ref.dtype)

    @pl.when(jnp.logical_and(j > n_copy, j < n_lx))
    def _():
        o_ref[...] = jnp.zeros(o_ref.shape, o_ref.dtype)

    @pl.when(j >= n_lx)
    def _():
        o_ref[...] = jnp.concatenate([a_ref[:, shift:], b_ref[:, :shift]], axis=1).astype(o_ref.dtype)


def _proj_weight(w_in, l, lay, dm):
    _, d, n_in = w_in.shape
    real_attn = lay.krope + dm.rope_dim
    bw = 1024
    while lay.krope % bw or lay.lx % bw:
        bw //= 2
    assert bw >= LANES and dm.rope_dim < bw
    n_copy = lay.krope // bw
    n_lx = lay.lx // bw
    last_in = (n_in - 1) // bw
    tr = _tile(d, 512, SUBLANES)

    def a_idx(j):
        return jnp.clip(n_copy + j - n_lx, 0, last_in)

    def b_idx(j):
        return jnp.minimum(jnp.where(j < n_lx, j, n_copy + j - n_lx + 1), last_in)

    return pl.pallas_call(
        functools.partial(_proj_weight_kernel, n_copy=n_copy, n_lx=n_lx, shift=real_attn - lay.krope),
        grid=(d // tr, lay.total // bw),
        in_specs=[pl.BlockSpec((None, tr, bw), lambda i, j: (l, i, a_idx(j))),
                  pl.BlockSpec((None, tr, bw), lambda i, j: (l, i, b_idx(j)))],
        out_specs=pl.BlockSpec((tr, bw), lambda i, j: (i, j)),
        out_shape=jax.ShapeDtypeStruct((d, lay.total), BF16),
        compiler_params=_params(2),
        name="proj_weight",
    )(w_in, w_in)


def _layer_weights(l, dm, lay, w_ff1_in, w_ff1_out, w_ff2_in, w_ff2_out, w_in, w_mla_uq, w_mla_ukv,
                   w_o_gqa, w_o_mla, w_o_lru, w_out):
    w_proj = _proj_weight(w_in, l, lay, dm)
    h = dm.mla_heads
    uq = w_mla_uq[l].reshape(dm.q_rank, h, LANES + dm.rope_dim)
    uq_rope = jnp.pad(uq[:, :, LANES:], ((0, 0), (0, 0), (0, LANES - dm.rope_dim)))
    w_uq = jnp.concatenate([uq[:, :, :LANES].reshape(dm.q_rank, h * LANES),
                            uq_rope.reshape(dm.q_rank, h * LANES)], axis=1).astype(BF16)
    ukv = w_mla_ukv[l].reshape(dm.kv_rank, h, 2 * LANES)
    w_ukv = jnp.concatenate([ukv[:, :, :LANES].reshape(dm.kv_rank, h * LANES),
                             ukv[:, :, LANES:].reshape(dm.kv_rank, h * LANES)], axis=1).astype(BF16)
    cast = functools.partial(_cast_layer, l=l)
    return dict(ff1_in=cast(w_ff1_in), ff1_out=cast(w_ff1_out), ff2_in=cast(w_ff2_in), ff2_out=cast(w_ff2_out),
                proj=w_proj, uq=w_uq, ukv=w_ukv, o_gqa=cast(w_o_gqa), o_mla=cast(w_o_mla),
                o_lru=cast(w_o_lru), out=cast(w_out))


def _forward(dm, x, c, ctx, c_ctx, w_mod, b_mod, g_norm, w_ff1_in, w_ff1_out, w_ff2_in, w_ff2_out, w_in,
             gqa_q_norm, gqa_k_norm, mla_q_norm, mla_kv_norm, w_mla_uq, w_mla_ukv, lru_conv_w, lru_conv_b,
             lru_wa, lru_ba, lru_wi, lru_bi, lru_lambda, w_o_gqa, w_o_mla, w_o_lru, w_out, final_norm):
    b, tl, d = x.shape
    tc = ctx.shape[1]
    tt = tl + tc
    depth = w_mod.shape[0]
    assert b == 2 and tl % ROW_TILE == 0 and tc % ROW_TILE == 0
    lay = _proj_layout(dm, d)
    tabs_g = _rope_tables(tl, LANES, dm.grid_w, tc)
    tabs_m = _rope_tables(tl, dm.rope_dim, dm.grid_w, tc)

    c_rows = jnp.concatenate([c, c_ctx[None, :], jnp.zeros((SUBLANES - b - 1, d), F32)], axis=0)
    mods_all = _mod_vectors(c_rows, w_mod, b_mod)

    xc = ctx
    for l in range(depth):
        last = l == depth - 1
        w = _layer_weights(l, dm, lay, w_ff1_in, w_ff1_out, w_ff2_in, w_ff2_out, w_in, w_mla_uq, w_mla_ukv,
                           w_o_gqa, w_o_mla, w_o_lru, w_out)
        mods = mods_all[l, :b + 1].reshape((b + 1) * N_MOD, 1, d)

        h1 = _adaln_norm(x, xc, g_norm[l, 0], mods, 0, 1, True)
        act = _swiglu_in(h1.reshape(b * tt, d), w['ff1_in']).reshape(b, tt, dm.d_ff)
        x = _residual_matmul(act, w['ff1_out'], x, mods, 2, 0.5, 0, False)
        xc = _residual_matmul(act, w['ff1_out'], xc, mods, 2, 0.5, tl, True)

        h2 = _adaln_norm(x, xc, g_norm[l, 1], mods, 3, 4, True)
        proj = _matmul(h2.reshape(b * tt, d), w['proj']).reshape(b, tt, lay.total)
        k_gqa, cq_n, ckv_n, k_rope = _prep(proj, lay, dm, tabs_g, tabs_m, gqa_k_norm[l], mla_q_norm[l],
                                           mla_kv_norm[l])
        q_up = _matmul(cq_n.reshape(b * tt, dm.q_rank), w['uq']).reshape(b, tt, -1)
        kv_up = _matmul(ckv_n.reshape(b * tt, dm.kv_rank), w['ukv']).reshape(b, tt, -1)
        y_lru = _lru(proj, lay, dm, tl, lru_conv_w[l], lru_conv_b[l], lru_wa[l], lru_ba[l], lru_wi[l],
                     lru_bi[l], lru_lambda[l])

        o_gqa = _gqa_attention(proj, k_gqa, lay, dm, tabs_g, gqa_q_norm[l], tl, 0, tt, 0)
        o_mla = _mla_attention(q_up, kv_up, k_rope, dm, tabs_m, tl, 0, tt, 0)
        merged = _merge(o_gqa, o_mla, y_lru, proj, w['o_gqa'], w['o_mla'], w['o_lru'], lay.gate, tl, 0)
        x = _residual_matmul(merged, w['out'], x, mods, 5, None, 0, False)
        if not last:
            o_gqa_c = _gqa_attention(proj, k_gqa, lay, dm, tabs_g, gqa_q_norm[l], tc, tl, tc, tl)
            o_mla_c = _mla_attention(q_up, kv_up, k_rope, dm, tabs_m, tc, tl, tc, tl)
            merged_c = _merge(o_gqa_c, o_mla_c, y_lru, proj, w['o_gqa'], w['o_mla'], w['o_lru'],
                              lay.gate, tc, tl)
            xc = _residual_matmul(merged_c, w['out'], xc, mods, 5, None, 0, True)

        h3 = _adaln_norm(x, xc, g_norm[l, 2], mods, 6, 7, not last)
        rows = h3.shape[1]
        act = _swiglu_in(h3.reshape(b * rows, d), w['ff2_in']).reshape(b, rows, dm.d_ff)
        x = _residual_matmul(act, w['ff2_out'], x, mods, 8, 0.5, 0, False)
        if not last:
            xc = _residual_matmul(act, w['ff2_out'], xc, mods, 8, 0.5, tl, True)
    return _final_norm(x, final_norm)


def kernel(x, c, ctx, c_ctx, w_mod, b_mod, g_norm, w_ff1_in, w_ff1_out, w_ff2_in, w_ff2_out, w_in, gqa_q_norm, gqa_k_norm, mla_q_norm, mla_kv_norm, w_mla_uq, w_mla_ukv, lru_conv_w, lru_conv_b, lru_wa, lru_ba, lru_wi, lru_bi, lru_lambda, w_o_gqa, w_o_mla, w_o_lru, w_out, final_norm):
    return _forward(DIMS, x, c, ctx, c_ctx, w_mod, b_mod, g_norm, w_ff1_in, w_ff1_out, w_ff2_in, w_ff2_out,
                    w_in, gqa_q_norm, gqa_k_norm, mla_q_norm, mla_kv_norm, w_mla_uq, w_mla_ukv, lru_conv_w,
                    lru_conv_b, lru_wa, lru_ba, lru_wi, lru_bi, lru_lambda, w_o_gqa, w_o_mla, w_o_lru, w_out,
                    final_norm)
```

```python
import dataclasses
import functools

import jax
import jax.numpy as jnp
from jax import lax
from jax.experimental import pallas as pl
from jax.experimental.pallas import tpu as pltpu

F32 = jnp.float32
BF16 = jnp.bfloat16

LANES = 128
SUBLANES = 8
VMEM_LIMIT = 60 * 1024 * 1024

ROPE_THETA = 10000.0
NORM_EPS = 1e-6
N_MOD = 9
LRU_C = 8.0
CONV_WIDTH = 4
ROW_TILE = 256


@dataclasses.dataclass(frozen=True)
class Dims:
    grid_w: int = 64
    gqa_heads: int = 16
    gqa_kv: int = 4
    mla_heads: int = 16
    q_rank: int = 1536
    kv_rank: int = 512
    rope_dim: int = 64
    lru_w: int = 2048
    d_ff: int = 6144
    proj_pad: int = 512


DIMS = Dims()


def _params(n_grid):
    return pltpu.CompilerParams(dimension_semantics=("arbitrary",) * n_grid,
                                vmem_limit_bytes=VMEM_LIMIT)


def _tile(dim, pref, align):
    best = None
    t = align
    while t <= min(dim, pref):
        if dim % t == 0:
            best = t
        t += align
    return best if best is not None else dim


@dataclasses.dataclass(frozen=True)
class ProjLayout:
    q: int
    k: int
    v: int
    cq: int
    ckv: int
    krope: int
    lx: int
    lg: int
    gate: int
    total: int


def _proj_layout(dm, d_model):
    q = 0
    k = q + dm.gqa_heads * LANES
    v = k + dm.gqa_kv * LANES
    cq = v + dm.gqa_kv * LANES
    ckv = cq + dm.q_rank
    krope = ckv + dm.kv_rank
    attn_end = krope + LANES
    lx = -(-attn_end // dm.proj_pad) * dm.proj_pad
    lg = lx + dm.lru_w
    gate = lg + dm.lru_w
    total = gate + 3 * d_model
    return ProjLayout(q, k, v, cq, ckv, krope, lx, lg, gate, total)


def _mod_kernel(c_ref, w_ref, b_ref, o_ref):
    c = c_ref[...]
    a = (c * jax.nn.sigmoid(c)).astype(BF16)
    acc = jnp.dot(a, w_ref[...].astype(BF16), preferred_element_type=F32)
    o_ref[...] = acc + b_ref[...]


def _mod_vectors(c_rows, w_mod, b_mod):
    n_layers, d, n = w_mod.shape
    tn = _tile(n, 512, LANES)
    return pl.pallas_call(
        _mod_kernel,
        grid=(n_layers, n // tn),
        in_specs=[
            pl.BlockSpec((SUBLANES, d), lambda l, j: (0, 0)),
            pl.BlockSpec((None, d, tn), lambda l, j: (l, 0, j)),
            pl.BlockSpec((None, 1, tn), lambda l, j: (l, 0, j)),
        ],
        out_specs=pl.BlockSpec((None, SUBLANES, tn), lambda l, j: (l, 0, j)),
        out_shape=jax.ShapeDtypeStruct((n_layers, SUBLANES, n), F32),
        compiler_params=_params(2),
        name="mod_vectors",
    )(c_rows, w_mod, b_mod.reshape(n_layers, 1, n))


def _rms(x, g):
    return x * lax.rsqrt(jnp.mean(x * x, axis=-1, keepdims=True) + NORM_EPS) * g


def _adaln_kernel(xl_ref, xc_ref, g_ref, shift_ref, scale_ref, o_ref, *, n_lat):
    def emit(x_ref):
        y = _rms(x_ref[...], g_ref[...])
        o_ref[...] = (y * (1.0 + scale_ref[0]) + shift_ref[0]).astype(o_ref.dtype)

    t = pl.program_id(1)
    pl.when(t < n_lat)(lambda: emit(xl_ref))
    pl.when(t >= n_lat)(lambda: emit(xc_ref))


def _adaln_norm(x_lat, x_ctx, g, mods, idx_shift, idx_scale, with_ctx):
    b, tl, d = x_lat.shape
    tc = x_ctx.shape[1]
    n_lat = tl // ROW_TILE
    n_ctx = tc // ROW_TILE if with_ctx else 0
    n_rows = tl + (tc if with_ctx else 0)

    def mod_row(bi, t):
        return jnp.where(t < n_lat, bi, 2) * N_MOD

    return pl.pallas_call(
        functools.partial(_adaln_kernel, n_lat=n_lat),
        grid=(b, n_lat + n_ctx),
        in_specs=[
            pl.BlockSpec((None, ROW_TILE, d), lambda bi, t: (bi, jnp.minimum(t, n_lat - 1), 0)),
            pl.BlockSpec((None, ROW_TILE, d), lambda bi, t: (bi, jnp.maximum(t - n_lat, 0), 0)),
            pl.BlockSpec((1, d), lambda bi, t: (0, 0)),
            pl.BlockSpec((1, 1, d), lambda bi, t: (mod_row(bi, t) + idx_shift, 0, 0)),
            pl.BlockSpec((1, 1, d), lambda bi, t: (mod_row(bi, t) + idx_scale, 0, 0)),
        ],
        out_specs=pl.BlockSpec((None, ROW_TILE, d), lambda bi, t: (bi, t, 0)),
        out_shape=jax.ShapeDtypeStruct((b, n_rows, d), BF16),
        compiler_params=_params(2),
        name="adaln_norm",
    )(x_lat, x_ctx, g.reshape(1, d), mods, mods)


def _final_norm_kernel(x_ref, g_ref, o_ref):
    o_ref[...] = _rms(x_ref[...], g_ref[...])


def _final_norm(x, g):
    b, t, d = x.shape
    return pl.pallas_call(
        _final_norm_kernel,
        grid=(b, t // ROW_TILE),
        in_specs=[pl.BlockSpec((None, ROW_TILE, d), lambda bi, i: (bi, i, 0)),
                  pl.BlockSpec((1, d), lambda bi, i: (0, 0))],
        out_specs=pl.BlockSpec((None, ROW_TILE, d), lambda bi, i: (bi, i, 0)),
        out_shape=jax.ShapeDtypeStruct((b, t, d), F32),
        compiler_params=_params(2),
        name="final_norm",
    )(x, g.reshape(1, d))


def _mm_kernel(a_ref, b_ref, o_ref):
    o_ref[...] = jnp.dot(a_ref[...], b_ref[...], preferred_element_type=F32).astype(o_ref.dtype)


def _matmul(a, w, tm_pref=1536, tn_pref=512):
    m, k = a.shape
    n = w.shape[1]
    tm = _tile(m, tm_pref, ROW_TILE)
    tn = _tile(n, tn_pref, LANES)
    return pl.pallas_call(
        _mm_kernel,
        grid=(m // tm, n // tn),
        in_specs=[pl.BlockSpec((tm, k), lambda i, j: (i, 0)),
                  pl.BlockSpec((k, tn), lambda i, j: (0, j))],
        out_specs=pl.BlockSpec((tm, tn), lambda i, j: (i, j)),
        out_shape=jax.ShapeDtypeStruct((m, n), BF16),
        compiler_params=_params(2),
        name="matmul",
    )(a, w)


def _swiglu_kernel(a_ref, wg_ref, wu_ref, o_ref):
    a = a_ref[...]
    g = jnp.dot(a, wg_ref[...], preferred_element_type=F32)
    u = jnp.dot(a, wu_ref[...], preferred_element_type=F32)
    o_ref[...] = (g * jax.nn.sigmoid(g) * u).astype(o_ref.dtype)


def _swiglu_in(h, w_in, tm_pref=1536, tn_pref=512):
    m, k = h.shape
    f = w_in.shape[1] // 2
    tm = _tile(m, tm_pref, ROW_TILE)
    tn = _tile(f, tn_pref, LANES)
    nj = f // tn
    return pl.pallas_call(
        _swiglu_kernel,
        grid=(m // tm, nj),
        in_specs=[pl.BlockSpec((tm, k), lambda i, j: (i, 0)),
                  pl.BlockSpec((k, tn), lambda i, j: (0, j)),
                  pl.BlockSpec((k, tn), lambda i, j: (0, j + nj))],
        out_specs=pl.BlockSpec((tm, tn), lambda i, j: (i, j)),
        out_shape=jax.ShapeDtypeStruct((m, f), BF16),
        compiler_params=_params(2),
        name="swiglu_in",
    )(h, w_in, w_in)


def _residual_kernel(a_ref, w_ref, x_ref, gate_ref, o_ref, *, coef):
    y = jnp.dot(a_ref[...], w_ref[...], preferred_element_type=F32)
    gate = gate_ref[0] if coef is None else coef * gate_ref[0]
    o_ref[...] = x_ref[...] + gate * y


def _residual_matmul(a, w, x, mods, idx_gate, coef, a_row_off, is_ctx, tm_pref=1024, tn_pref=512):
    b, r, n = x.shape
    k = a.shape[2]
    tm = _tile(r, tm_pref, ROW_TILE)
    tn = _tile(n, tn_pref, LANES)
    off = a_row_off // tm

    def gate_row(bi):
        return (2 if is_ctx else bi) * N_MOD + idx_gate

    return pl.pallas_call(
        functools.partial(_residual_kernel, coef=coef),
        grid=(b, r // tm, n // tn),
        in_specs=[pl.BlockSpec((None, tm, k), lambda bi, i, j: (bi, i + off, 0)),
                  pl.BlockSpec((k, tn), lambda bi, i, j: (0, j)),
                  pl.BlockSpec((None, tm, tn), lambda bi, i, j: (bi, i, j)),
                  pl.BlockSpec((1, 1, tn), lambda bi, i, j: (gate_row(bi), 0, j))],
        out_specs=pl.BlockSpec((None, tm, tn), lambda bi, i, j: (bi, i, j)),
        out_shape=jax.ShapeDtypeStruct((b, r, n), F32),
        compiler_params=_params(3),
        name="residual_matmul",
    )(a, w, x, mods)


def _merge_kernel(a0_ref, a1_ref, a2_ref, w0_ref, w1_ref, w2_ref, g0_ref, g1_ref, g2_ref, o_ref):
    def branch(a_ref, w_ref, g_ref):
        y = jnp.dot(a_ref[...], w_ref[...], preferred_element_type=F32)
        return jax.nn.sigmoid(g_ref[...].astype(F32)) * y

    acc = branch(a0_ref, w0_ref, g0_ref) + branch(a1_ref, w1_ref, g1_ref)
    o_ref[...] = (acc + branch(a2_ref, w2_ref, g2_ref)).astype(o_ref.dtype)


def _merge(o_gqa, o_mla, y_lru, proj, w0, w1, w2, gate_col, rows, mix_row_off, tm_pref=1024, tn_pref=512):
    b = o_gqa.shape[0]
    n = w0.shape[1]
    tm = _tile(rows, tm_pref, ROW_TILE)
    tn = _tile(n, tn_pref, LANES)
    off = mix_row_off // tm
    gcol = gate_col // tn
    nj = n // tn

    def a_spec(a, row_off):
        return pl.BlockSpec((None, tm, a.shape[2]), lambda bi, i, j: (bi, i + row_off, 0))

    def w_spec(w):
        return pl.BlockSpec((w.shape[0], tn), lambda bi, i, j: (0, j))

    def gate_spec(br):
        return pl.BlockSpec((None, tm, tn), lambda bi, i, j: (bi, i + off, gcol + br * nj + j))

    return pl.pallas_call(
        _merge_kernel,
        grid=(b, rows // tm, nj),
        in_specs=[a_spec(o_gqa, 0), a_spec(o_mla, 0), a_spec(y_lru, off), w_spec(w0), w_spec(w1), w_spec(w2),
                  gate_spec(0), gate_spec(1), gate_spec(2)],
        out_specs=pl.BlockSpec((None, tm, tn), lambda bi, i, j: (bi, i, j)),
        out_shape=jax.ShapeDtypeStruct((b, rows, n), BF16),
        compiler_params=_params(3),
        name="merge",
    )(o_gqa, o_mla, y_lru, w0, w1, w2, proj, proj, proj)


def _rope_tables(n_tok, rot_dim, grid_w, ctx_len):
    rows = n_tok // grid_w
    r_idx, c_idx = jnp.meshgrid(jnp.arange(rows), jnp.arange(grid_w), indexing='ij')
    r_idx = r_idx.reshape(-1).astype(F32)
    c_idx = c_idx.reshape(-1).astype(F32)
    n_pairs = rot_dim // 4
    freqs = jnp.power(ROPE_THETA, -jnp.arange(n_pairs, dtype=F32) / n_pairs)
    ang = jnp.concatenate([r_idx[:, None] * freqs, c_idx[:, None] * freqs], axis=-1)
    cos, sin = jnp.cos(ang), jnp.sin(ang)
    cos_full = jnp.concatenate([cos, cos], axis=-1)
    sin_signed = jnp.concatenate([-sin, sin], axis=-1)
    reps = LANES // rot_dim
    cos_full = jnp.tile(cos_full, (1, reps))
    sin_signed = jnp.tile(sin_signed, (1, reps))
    cos_full = jnp.concatenate([cos_full, jnp.ones((ctx_len, LANES), F32)], axis=0)
    sin_signed = jnp.concatenate([sin_signed, jnp.zeros((ctx_len, LANES), F32)], axis=0)
    return cos_full, sin_signed


def _rope128(x, cos, sin):
    return x * cos + pltpu.roll(x, LANES // 2, 1) * sin


def _rope64(x, cos, sin):
    lane = lax.broadcasted_iota(jnp.int32, x.shape, 1)
    first_half = (lane % 64) < 32
    partner = jnp.where(first_half, pltpu.roll(x, LANES - 32, 1), pltpu.roll(x, 32, 1))
    return x * cos + partner * sin


def _prep_kernel(k_ref, cq_ref, ckv_ref, kr_ref, cg_ref, sg_ref, cm_ref, sm_ref,
                 gk_ref, gq_ref, gkv_ref, ko_ref, cqo_ref, ckvo_ref, kro_ref, *, kv_heads):
    cos_g, sin_g = cg_ref[...], sg_ref[...]
    for h in range(kv_heads):
        cols = slice(h * LANES, (h + 1) * LANES)
        y = _rms(k_ref[:, cols].astype(F32), gk_ref[...])
        ko_ref[:, cols] = _rope128(y, cos_g, sin_g).astype(BF16)
    cqo_ref[...] = _rms(cq_ref[...].astype(F32), gq_ref[...]).astype(BF16)
    ckvo_ref[...] = _rms(ckv_ref[...].astype(F32), gkv_ref[...]).astype(BF16)
    kro_ref[...] = _rope64(kr_ref[...].astype(F32), cm_ref[...], sm_ref[...]).astype(BF16)


def _prep(proj, lay, dm, tabs_g, tabs_m, gk, gq, gkv):
    b, tt, _ = proj.shape
    kw = dm.gqa_kv * LANES
    assert lay.k % kw == 0 and lay.cq % dm.q_rank == 0 and lay.ckv % dm.kv_rank == 0
    row = lambda bi, t: (bi, t, 0)
    tab = pl.BlockSpec((ROW_TILE, LANES), lambda bi, t: (t, 0))

    def vec(n):
        return pl.BlockSpec((1, n), lambda bi, t: (0, 0))

    return pl.pallas_call(
        functools.partial(_prep_kernel, kv_heads=dm.gqa_kv),
        grid=(b, tt // ROW_TILE),
        in_specs=[pl.BlockSpec((None, ROW_TILE, kw), lambda bi, t: (bi, t, lay.k // kw)),
                  pl.BlockSpec((None, ROW_TILE, dm.q_rank), lambda bi, t: (bi, t, lay.cq // dm.q_rank)),
                  pl.BlockSpec((None, ROW_TILE, dm.kv_rank), lambda bi, t: (bi, t, lay.ckv // dm.kv_rank)),
                  pl.BlockSpec((None, ROW_TILE, LANES), lambda bi, t: (bi, t, lay.krope // LANES)),
                  tab, tab, tab, tab, vec(LANES), vec(dm.q_rank), vec(dm.kv_rank)],
        out_specs=[pl.BlockSpec((None, ROW_TILE, kw), row),
                   pl.BlockSpec((None, ROW_TILE, dm.q_rank), row),
                   pl.BlockSpec((None, ROW_TILE, dm.kv_rank), row),
                   pl.BlockSpec((None, ROW_TILE, LANES), row)],
        out_shape=[jax.ShapeDtypeStruct((b, tt, kw), BF16),
                   jax.ShapeDtypeStruct((b, tt, dm.q_rank), BF16),
                   jax.ShapeDtypeStruct((b, tt, dm.kv_rank), BF16),
                   jax.ShapeDtypeStruct((b, tt, LANES), BF16)],
        compiler_params=_params(2),
        name="prep",
    )(proj, proj, proj, proj, tabs_g[0], tabs_g[1], tabs_m[0], tabs_m[1],
      gk.reshape(1, LANES), gq.reshape(1, dm.q_rank), gkv.reshape(1, dm.kv_rank))


LOG2E = 1.4426950408889634
KEY_CHUNK = 768


SCORE_BUFFERS = 4
ONES_ROWS = 16


def _flash_scratch(dk, n_q, key_rows, tk):
    return ([pltpu.VMEM((dk, n_q), BF16),
             pltpu.VMEM((key_rows // tk, LANES + ONES_ROWS, tk), BF16),
             pltpu.VMEM((1, n_q), F32),
             pltpu.VMEM((LANES + ONES_ROWS, n_q), F32)]
            + [pltpu.VMEM((tk, n_q), F32)] * SCORE_BUFFERS
            + [pltpu.VMEM((1, n_q), F32)] * SCORE_BUFFERS)


def _flash_t(qt_ref, load_k, load_vt, n_chunks, stats):
    m_ref, acc_ref = stats[:2]
    s_refs = stats[2:2 + SCORE_BUFFERS]
    smax_refs = stats[2 + SCORE_BUFFERS:]
    m_ref[...] = jnp.full(m_ref.shape, -jnp.inf, F32)
    acc_ref[...] = jnp.zeros(acc_ref.shape, F32)

    def scores(c, slot):
        s = jnp.dot(load_k(c), qt_ref[...], preferred_element_type=F32)
        s_refs[slot][...] = s
        smax_refs[slot][...] = jnp.max(s, axis=0, keepdims=True)

    def update(c, slot):
        m_prev = m_ref[...]
        m_new = jnp.maximum(m_prev, smax_refs[slot][...])
        p = jnp.exp2(s_refs[slot][...] - m_new)
        alpha = jnp.exp2(m_prev - m_new)
        acc_ref[...] = alpha * acc_ref[...] + jnp.dot(load_vt(c), p.astype(BF16),
                                                     preferred_element_type=F32)
        m_ref[...] = m_new

    scores(0, 0)
    n_main = (n_chunks - 1) // SCORE_BUFFERS

    def body(i, carry):
        c = SCORE_BUFFERS * i
        for j in range(SCORE_BUFFERS):
            scores(c + j + 1, (j + 1) % SCORE_BUFFERS)
            update(c + j, j)
        return carry

    lax.fori_loop(0, n_main, body, 0)
    for c in range(SCORE_BUFFERS * n_main, n_chunks):
        if c + 1 < n_chunks:
            scores(c + 1, (c + 1) % SCORE_BUFFERS)
        update(c, c % SCORE_BUFFERS)
    return acc_ref[:LANES, :] / acc_ref[LANES:LANES + 1, :]


def _stage_values_t(v_ref, vt_ref, tk):
    @pl.when(pl.program_id(2) == 0)
    def _():
        for c in range(vt_ref.shape[0]):
            vt_ref[c, :LANES, :] = v_ref[c * tk:(c + 1) * tk, :].astype(F32).T.astype(BF16)
            vt_ref[c, LANES:, :] = jnp.ones((ONES_ROWS, tk), BF16)


def _gqa_kernel(q_ref, k_ref, v_ref, cos_ref, sin_ref, gq_ref, o_ref, qt_ref, vt_ref, *stats,
                group, tq, tk, scale):
    _stage_values_t(v_ref, vt_ref, tk)
    cos, sin = cos_ref[...], sin_ref[...]
    for g in range(group):
        y = _rms(q_ref[:, g * LANES:(g + 1) * LANES].astype(F32), gq_ref[...])
        qt_ref[:, g * tq:(g + 1) * tq] = (_rope128(y, cos, sin) * scale).T.astype(BF16)
    out_t = _flash_t(qt_ref,
                     lambda c: k_ref[pl.ds(pl.multiple_of(c * tk, tk), tk), :],
                     lambda c: vt_ref[c],
                     vt_ref.shape[0], stats)
    for g in range(group):
        o_ref[:, g * LANES:(g + 1) * LANES] = out_t[:, g * tq:(g + 1) * tq].T.astype(o_ref.dtype)


def _gqa_attention(proj, k_prep, lay, dm, tabs, gq, q_rows, q_off, key_rows, key_off, tq_pref=512):
    b = proj.shape[0]
    group = dm.gqa_heads // dm.gqa_kv
    gw = group * LANES
    tq = _tile(q_rows, tq_pref, ROW_TILE)
    tk = _tile(key_rows, KEY_CHUNK, ROW_TILE)
    assert q_off % tq == 0 and key_off % key_rows == 0
    qo, ko, vcol = q_off // tq, key_off // key_rows, lay.v // LANES
    return pl.pallas_call(
        functools.partial(_gqa_kernel, group=group, tq=tq, tk=tk, scale=float(LANES) ** -0.5 * LOG2E),
        grid=(b, dm.gqa_kv, q_rows // tq),
        in_specs=[pl.BlockSpec((None, tq, gw), lambda bi, h, i: (bi, i + qo, h)),
                  pl.BlockSpec((None, key_rows, LANES), lambda bi, h, i: (bi, ko, h)),
                  pl.BlockSpec((None, key_rows, LANES), lambda bi, h, i: (bi, ko, vcol + h)),
                  pl.BlockSpec((tq, LANES), lambda bi, h, i: (i + qo, 0)),
                  pl.BlockSpec((tq, LANES), lambda bi, h, i: (i + qo, 0)),
                  pl.BlockSpec((1, LANES), lambda bi, h, i: (0, 0))],
        out_specs=pl.BlockSpec((None, tq, gw), lambda bi, h, i: (bi, i, h)),
        out_shape=jax.ShapeDtypeStruct((b, q_rows, dm.gqa_heads * LANES), BF16),
        scratch_shapes=_flash_scratch(LANES, group * tq, key_rows, tk),
        compiler_params=_params(3),
        name="gqa_attention",
    )(proj, k_prep, proj, tabs[0], tabs[1], gq.reshape(1, LANES))


def _mla_kernel(qn_ref, qr_ref, kn_ref, kr_ref, v_ref, cos_ref, sin_ref, o_ref,
                qt_ref, vt_ref, *stats, tk, scale):
    _stage_values_t(v_ref, vt_ref, tk)
    qt_ref[:LANES, :] = (qn_ref[...].astype(F32) * scale).T.astype(BF16)
    qr = _rope64(qr_ref[...].astype(F32), cos_ref[...], sin_ref[...])
    qt_ref[LANES:, :] = (qr * scale).T.astype(BF16)

    def load_k(c):
        rows = pl.ds(pl.multiple_of(c * tk, tk), tk)
        return jnp.concatenate([kn_ref[rows, :], kr_ref[rows, :]], axis=1)

    out_t = _flash_t(qt_ref, load_k, lambda c: vt_ref[c], vt_ref.shape[0], stats)
    o_ref[...] = out_t.T.astype(o_ref.dtype)


def _mla_attention(q_up, kv_up, k_rope, dm, tabs, q_rows, q_off, key_rows, key_off, tq_pref=2048):
    b = q_up.shape[0]
    h_n = dm.mla_heads
    tq = _tile(q_rows, tq_pref, ROW_TILE)
    tk = _tile(key_rows, KEY_CHUNK, ROW_TILE)
    assert q_off % tq == 0 and key_off % key_rows == 0
    qo, ko = q_off // tq, key_off // key_rows
    scale = float(LANES + dm.rope_dim) ** -0.5 * LOG2E
    return pl.pallas_call(
        functools.partial(_mla_kernel, tk=tk, scale=scale),
        grid=(b, h_n, q_rows // tq),
        in_specs=[pl.BlockSpec((None, tq, LANES), lambda bi, h, i: (bi, i + qo, h)),
                  pl.BlockSpec((None, tq, LANES), lambda bi, h, i: (bi, i + qo, h_n + h)),
                  pl.BlockSpec((None, key_rows, LANES), lambda bi, h, i: (bi, ko, h)),
                  pl.BlockSpec((None, key_rows, LANES), lambda bi, h, i: (bi, ko, 0)),
                  pl.BlockSpec((None, key_rows, LANES), lambda bi, h, i: (bi, ko, h_n + h)),
                  pl.BlockSpec((tq, LANES), lambda bi, h, i: (i + qo, 0)),
                  pl.BlockSpec((tq, LANES), lambda bi, h, i: (i + qo, 0))],
        out_specs=pl.BlockSpec((None, tq, LANES), lambda bi, h, i: (bi, i, h)),
        out_shape=jax.ShapeDtypeStruct((b, q_rows, h_n * LANES), BF16),
        scratch_shapes=_flash_scratch(2 * LANES, tq, key_rows, tk),
        compiler_params=_params(3),
        name="mla_attention",
    )(q_up, q_up, kv_up, k_rope, kv_up, tabs[0], tabs[1])


def _lru_kernel(lx_ref, lg_ref, cw_ref, cb_ref, wa_ref, ba_ref, wi_ref, bi_ref, lam_ref, o_ref,
                xs_ref, xc_ref, hf_ref, hb_ref, *, tl, tc, ch):
    tt = tl + tc
    n_lat, n_ctx = tl // ch, tc // ch
    pad = SUBLANES
    zeros = jnp.zeros((pad, LANES), F32)
    xs_ref[0:pad, :] = zeros
    xs_ref[pad + tt:2 * pad + tt, :] = zeros

    def stage(c, carry):
        r = pl.multiple_of(c * ch, ch)
        xs_ref[pl.ds(pad + r, ch), :] = lx_ref[pl.ds(r, ch), :].astype(F32)
        return carry

    lax.fori_loop(0, n_lat + n_ctx, stage, 0)

    def conv(c, carry):
        r = pl.multiple_of(c * ch, ch)
        ext = xs_ref[pl.ds(r, ch + 2 * pad), :]
        n_ext = ch + 2 * pad
        starts = jnp.logical_or(c == 0, c == n_lat)
        ends = jnp.logical_or(c == n_lat - 1, c == n_lat + n_ctx - 1)
        head = jnp.where(starts, 0.0, ext[:pad])
        tail = jnp.where(ends, 0.0, ext[pad + ch:])
        ext = jnp.concatenate([head, ext[pad:pad + ch], tail], axis=0)
        y = cb_ref[...]
        for j in range(CONV_WIDTH):
            shift = j - 1
            tap = ext if shift == 0 else pltpu.roll(ext, (-shift) % n_ext, 0)
            y = y + tap[pad:pad + ch] * cw_ref[j:j + 1, :]
        xc_ref[pl.ds(r, ch), :] = y
        return carry

    lax.fori_loop(0, n_lat + n_ctx, conv, 0)

    row = lax.broadcasted_iota(jnp.int32, (SUBLANES, LANES), 0)

    def coeffs(xc, d):
        xb = xc.astype(BF16)
        r = jax.nn.sigmoid(jnp.dot(xb, wa_ref[d].astype(BF16), preferred_element_type=F32) + ba_ref[d:d + 1, :])
        i = jax.nn.sigmoid(jnp.dot(xb, wi_ref[d].astype(BF16), preferred_element_type=F32) + bi_ref[d:d + 1, :])
        z = -lam_ref[d:d + 1, :]
        softplus = jnp.maximum(z, 0.0) + jnp.log1p(jnp.exp(-jnp.abs(z)))
        log_a = -LRU_C * r * softplus
        a = jnp.exp(log_a)
        u = jnp.sqrt(1.0 - a * a) * (i * xc)
        return a, u

    def scan_chunk(c, h, d, out_ref):
        r = pl.multiple_of(c * ch, ch)
        a, u = coeffs(xc_ref[pl.ds(r, ch), :], d)
        n_blk = ch // SUBLANES
        order = range(n_blk) if d == 0 else range(n_blk - 1, -1, -1)
        edge = SUBLANES - 1 if d == 0 else 0
        blocks = []
        for j in range(n_blk):
            aj = a[j * SUBLANES:(j + 1) * SUBLANES]
            uj = u[j * SUBLANES:(j + 1) * SUBLANES]
            for dist in (1, 2, 4):
                if d == 0:
                    shift, mask = dist, row >= dist
                else:
                    shift, mask = SUBLANES - dist, row < SUBLANES - dist
                a_s = jnp.where(mask, pltpu.roll(aj, shift, 0), 1.0)
                u_s = jnp.where(mask, pltpu.roll(uj, shift, 0), 0.0)
                uj = uj + aj * u_s
                aj = aj * a_s
            blocks.append((aj, uj))
        carry_in = [None] * n_blk
        for j in order:
            carry_in[j] = h
            aj, uj = blocks[j]
            h = aj[edge:edge + 1] * h + uj[edge:edge + 1]
        states = [blocks[j][0] * carry_in[j] + blocks[j][1] for j in range(n_blk)]
        out_ref[pl.ds(r, ch), :] = jnp.concatenate(states, axis=0)
        return h

    def both(fwd_chunk, bwd_chunk, carry):
        return (scan_chunk(fwd_chunk, carry[0], 0, hf_ref), scan_chunk(bwd_chunk, carry[1], 1, hb_ref))

    h0 = jnp.zeros((1, LANES), F32)
    carry = lax.fori_loop(0, n_ctx, lambda k, cr: both(n_lat + k, n_lat + n_ctx - 1 - k, cr), (h0, h0))
    lax.fori_loop(0, n_lat, lambda k, cr: both(k, n_lat - 1 - k, cr), carry)

    def combine(c, carry):
        rows = pl.ds(pl.multiple_of(c * ch, ch), ch)
        gate = jax.nn.gelu(lg_ref[rows, :].astype(F32))
        o_ref[rows, :] = ((hf_ref[rows, :] + hb_ref[rows, :]) * gate).astype(o_ref.dtype)
        return carry

    lax.fori_loop(0, n_lat + n_ctx, combine, 0)


def _lru(proj, lay, dm, tl, conv_w, conv_b, wa, ba, wi, bi, lam):
    b, tt, _ = proj.shape
    nb = dm.lru_w // LANES
    xcol, gcol = lay.lx // LANES, lay.lg // LANES
    vec2 = pl.BlockSpec((2, LANES), lambda bi_, n: (0, n))
    wspec = pl.BlockSpec((2, None, LANES, LANES), lambda bi_, n: (0, n, 0, 0))
    return pl.pallas_call(
        functools.partial(_lru_kernel, tl=tl, tc=tt - tl, ch=ROW_TILE),
        grid=(b, nb),
        in_specs=[pl.BlockSpec((None, tt, LANES), lambda bi_, n: (bi_, 0, xcol + n)),
                  pl.BlockSpec((None, tt, LANES), lambda bi_, n: (bi_, 0, gcol + n)),
                  pl.BlockSpec((CONV_WIDTH, LANES), lambda bi_, n: (0, n)),
                  pl.BlockSpec((1, LANES), lambda bi_, n: (0, n)),
                  wspec, vec2, wspec, vec2, vec2],
        out_specs=pl.BlockSpec((None, tt, LANES), lambda bi_, n: (bi_, 0, n)),
        out_shape=jax.ShapeDtypeStruct((b, tt, dm.lru_w), BF16),
        scratch_shapes=[pltpu.VMEM((tt + 2 * SUBLANES, LANES), F32),
                        pltpu.VMEM((tt, LANES), F32),
                        pltpu.VMEM((tt, LANES), F32),
                        pltpu.VMEM((tt, LANES), F32)],
        compiler_params=_params(2),
        name="rglru",
    )(proj, proj, conv_w, conv_b.reshape(1, dm.lru_w), wa, ba, wi, bi, lam)


def _cast_kernel(w_ref, o_ref):
    o_ref[...] = w_ref[...].astype(o_ref.dtype)


def _cast_layer(w, l):
    _, r, c = w.shape
    tr = _tile(r, 512, SUBLANES)
    tc = _tile(c, 2048, LANES)
    return pl.pallas_call(
        _cast_kernel,
        grid=(r // tr, c // tc),
        in_specs=[pl.BlockSpec((None, tr, tc), lambda i, j: (l, i, j))],
        out_specs=pl.BlockSpec((tr, tc), lambda i, j: (i, j)),
        out_shape=jax.ShapeDtypeStruct((r, c), BF16),
        compiler_params=_params(2),
        name="cast_weight",
    )(w)


def _layer_weights(l, dm, lay, w_ff1_in, w_ff1_out, w_ff2_in, w_ff2_out, w_in, w_mla_uq, w_mla_ukv,
                   w_o_gqa, w_o_mla, w_o_lru, w_out):
    d = w_in.shape[1]
    real_attn = lay.krope + dm.rope_dim
    w_proj = jnp.concatenate(
        [w_in[l][:, :real_attn].astype(BF16), jnp.zeros((d, lay.lx - real_attn), BF16),
         w_in[l][:, real_attn:].astype(BF16)], axis=1)
    h = dm.mla_heads
    uq = w_mla_uq[l].reshape(dm.q_rank, h, LANES + dm.rope_dim)
    uq_rope = jnp.pad(uq[:, :, LANES:], ((0, 0), (0, 0), (0, LANES - dm.rope_dim)))
    w_uq = jnp.concatenate([uq[:, :, :LANES].reshape(dm.q_rank, h * LANES),
                            uq_rope.reshape(dm.q_rank, h * LANES)], axis=1).astype(BF16)
    ukv = w_mla_ukv[l].reshape(dm.kv_rank, h, 2 * LANES)
    w_ukv = jnp.concatenate([ukv[:, :, :LANES].reshape(dm.kv_rank, h * LANES),
                             ukv[:, :, LANES:].reshape(dm.kv_rank, h * LANES)], axis=1).astype(BF16)
    cast = functools.partial(_cast_layer, l=l)
    return dict(ff1_in=cast(w_ff1_in), ff1_out=cast(w_ff1_out), ff2_in=cast(w_ff2_in), ff2_out=cast(w_ff2_out),
                proj=w_proj, uq=w_uq, ukv=w_ukv, o_gqa=cast(w_o_gqa), o_mla=cast(w_o_mla),
                o_lru=cast(w_o_lru), out=cast(w_out))


def _forward(dm, x, c, ctx, c_ctx, w_mod, b_mod, g_norm, w_ff1_in, w_ff1_out, w_ff2_in, w_ff2_out, w_in,
             gqa_q_norm, gqa_k_norm, mla_q_norm, mla_kv_norm, w_mla_uq, w_mla_ukv, lru_conv_w, lru_conv_b,
             lru_wa, lru_ba, lru_wi, lru_bi, lru_lambda, w_o_gqa, w_o_mla, w_o_lru, w_out, final_norm):
    b, tl, d = x.shape
    tc = ctx.shape[1]
    tt = tl + tc
    depth = w_mod.shape[0]
    assert b == 2 and tl % ROW_TILE == 0 and tc % ROW_TILE == 0
    lay = _proj_layout(dm, d)
    tabs_g = _rope_tables(tl, LANES, dm.grid_w, tc)
    tabs_m = _rope_tables(tl, dm.rope_dim, dm.grid_w, tc)

    c_rows = jnp.concatenate([c, c_ctx[None, :], jnp.zeros((SUBLANES - b - 1, d), F32)], axis=0)
    mods_all = _mod_vectors(c_rows, w_mod, b_mod)

    xc = ctx
    for l in range(depth):
        last = l == depth - 1
        w = _layer_weights(l, dm, lay, w_ff1_in, w_ff1_out, w_ff2_in, w_ff2_out, w_in, w_mla_uq, w_mla_ukv,
                           w_o_gqa, w_o_mla, w_o_lru, w_out)
        mods = mods_all[l, :b + 1].reshape((b + 1) * N_MOD, 1, d)

        h1 = _adaln_norm(x, xc, g_norm[l, 0], mods, 0, 1, True)
        act = _swiglu_in(h1.reshape(b * tt, d), w['ff1_in']).reshape(b, tt, dm.d_ff)
        x = _residual_matmul(act, w['ff1_out'], x, mods, 2, 0.5, 0, False)
        xc = _residual_matmul(act, w['ff1_out'], xc, mods, 2, 0.5, tl, True)

        h2 = _adaln_norm(x, xc, g_norm[l, 1], mods, 3, 4, True)
        proj = _matmul(h2.reshape(b * tt, d), w['proj']).reshape(b, tt, lay.total)
        k_gqa, cq_n, ckv_n, k_rope = _prep(proj, lay, dm, tabs_g, tabs_m, gqa_k_norm[l], mla_q_norm[l],
                                           mla_kv_norm[l])
        q_up = _matmul(cq_n.reshape(b * tt, dm.q_rank), w['uq']).reshape(b, tt, -1)
        kv_up = _matmul(ckv_n.reshape(b * tt, dm.kv_rank), w['ukv']).reshape(b, tt, -1)
        y_lru = _lru(proj, lay, dm, tl, lru_conv_w[l], lru_conv_b[l], lru_wa[l], lru_ba[l], lru_wi[l],
                     lru_bi[l], lru_lambda[l])

        o_gqa = _gqa_attention(proj, k_gqa, lay, dm, tabs_g, gqa_q_norm[l], tl, 0, tt, 0)
        o_mla = _mla_attention(q_up, kv_up, k_rope, dm, tabs_m, tl, 0, tt, 0)
        merged = _merge(o_gqa, o_mla, y_lru, proj, w['o_gqa'], w['o_mla'], w['o_lru'], lay.gate, tl, 0)
        x = _residual_matmul(merged, w['out'], x, mods, 5, None, 0, False)
        if not last:
            o_gqa_c = _gqa_attention(proj, k_gqa, lay, dm, tabs_g, gqa_q_norm[l], tc, tl, tc, tl)
            o_mla_c = _mla_attention(q_up, kv_up, k_rope, dm, tabs_m, tc, tl, tc, tl)
            merged_c = _merge(o_gqa_c, o_mla_c, y_lru, proj, w['o_gqa'], w['o_mla'], w['o_lru'],
                              lay.gate, tc, tl)
            xc = _residual_matmul(merged_c, w['out'], xc, mods, 5, None, 0, True)

        h3 = _adaln_norm(x, xc, g_norm[l, 2], mods, 6, 7, not last)
        rows = h3.shape[1]
        act = _swiglu_in(h3.reshape(b * rows, d), w['ff2_in']).reshape(b, rows, dm.d_ff)
        x = _residual_matmul(act, w['ff2_out'], x, mods, 8, 0.5, 0, False)
        if not last:
            xc = _residual_matmul(act, w['ff2_out'], xc, mods, 8, 0.5, tl, True)
    return _final_norm(x, final_norm)


def kernel(x, c, ctx, c_ctx, w_mod, b_mod, g_norm, w_ff1_in, w_ff1_out, w_ff2_in, w_ff2_out, w_in, gqa_q_norm, gqa_k_norm, mla_q_norm, mla_kv_norm, w_mla_uq, w_mla_ukv, lru_conv_w, lru_conv_b, lru_wa, lru_ba, lru_wi, lru_bi, lru_lambda, w_o_gqa, w_o_mla, w_o_lru, w_out, final_norm):
    return _forward(DIMS, x, c, ctx, c_ctx, w_mod, b_mod, g_norm, w_ff1_in, w_ff1_out, w_ff2_in, w_ff2_out,
                    w_in, gqa_q_norm, gqa_k_norm, mla_q_norm, mla_kv_norm, w_mla_uq, w_mla_ukv, lru_conv_w,
                    lru_conv_b, lru_wa, lru_ba, lru_wi, lru_bi, lru_lambda, w_o_gqa, w_o_mla, w_o_lru, w_out,
                    final_norm)
```

```python
import dataclasses
import functools

import jax
import jax.numpy as jnp
from jax import lax
from jax.experimental import pallas as pl
from jax.experimental.pallas import tpu as pltpu

F32 = jnp.float32
BF16 = jnp.bfloat16

LANES = 128
SUBLANES = 8
VMEM_LIMIT = 60 * 1024 * 1024

ROPE_THETA = 10000.0
NORM_EPS = 1e-6
N_MOD = 9
LRU_C = 8.0
CONV_WIDTH = 4
ROW_TILE = 256


@dataclasses.dataclass(frozen=True)
class Dims:
    grid_w: int = 64
    gqa_heads: int = 16
    gqa_kv: int = 4
    mla_heads: int = 16
    q_rank: int = 1536
    kv_rank: int = 512
    rope_dim: int = 64
    lru_w: int = 2048
    d_ff: int = 6144
    proj_pad: int = 512


DIMS = Dims()


def _params(n_grid):
    return pltpu.CompilerParams(dimension_semantics=("arbitrary",) * n_grid,
                                vmem_limit_bytes=VMEM_LIMIT)


def _tile(dim, pref, align):
    best = None
    t = align
    while t <= min(dim, pref):
        if dim % t == 0:
            best = t
        t += align
    return best if best is not None else dim


@dataclasses.dataclass(frozen=True)
class ProjLayout:
    q: int
    k: int
    v: int
    cq: int
    ckv: int
    krope: int
    lx: int
    lg: int
    gate: int
    total: int


def _proj_layout(dm, d_model):
    q = 0
    k = q + dm.gqa_heads * LANES
    v = k + dm.gqa_kv * LANES
    cq = v + dm.gqa_kv * LANES
    ckv = cq + dm.q_rank
    krope = ckv + dm.kv_rank
    attn_end = krope + LANES
    lx = -(-attn_end // dm.proj_pad) * dm.proj_pad
    lg = lx + dm.lru_w
    gate = lg + dm.lru_w
    total = gate + 3 * d_model
    return ProjLayout(q, k, v, cq, ckv, krope, lx, lg, gate, total)


def _mod_kernel(c_ref, w_ref, b_ref, o_ref):
    c = c_ref[...]
    a = (c * jax.nn.sigmoid(c)).astype(BF16)
    acc = jnp.dot(a, w_ref[...].astype(BF16), preferred_element_type=F32)
    o_ref[...] = acc + b_ref[...]


def _mod_vectors(c_rows, w_mod, b_mod):
    n_layers, d, n = w_mod.shape
    tn = _tile(n, 512, LANES)
    return pl.pallas_call(
        _mod_kernel,
        grid=(n_layers, n // tn),
        in_specs=[
            pl.BlockSpec((SUBLANES, d), lambda l, j: (0, 0)),
            pl.BlockSpec((None, d, tn), lambda l, j: (l, 0, j)),
            pl.BlockSpec((None, 1, tn), lambda l, j: (l, 0, j)),
        ],
        out_specs=pl.BlockSpec((None, SUBLANES, tn), lambda l, j: (l, 0, j)),
        out_shape=jax.ShapeDtypeStruct((n_layers, SUBLANES, n), F32),
        compiler_params=_params(2),
        name="mod_vectors",
    )(c_rows, w_mod, b_mod.reshape(n_layers, 1, n))


def _rms(x, g):
    return x * lax.rsqrt(jnp.mean(x * x, axis=-1, keepdims=True) + NORM_EPS) * g


def _adaln_kernel(xl_ref, xc_ref, g_ref, shift_ref, scale_ref, o_ref, *, n_lat):
    def emit(x_ref):
        y = _rms(x_ref[...], g_ref[...])
        o_ref[...] = (y * (1.0 + scale_ref[0]) + shift_ref[0]).astype(o_ref.dtype)

    t = pl.program_id(1)
    pl.when(t < n_lat)(lambda: emit(xl_ref))
    pl.when(t >= n_lat)(lambda: emit(xc_ref))


def _adaln_norm(x_lat, x_ctx, g, mods, idx_shift, idx_scale, with_ctx):
    b, tl, d = x_lat.shape
    tc = x_ctx.shape[1]
    n_lat = tl // ROW_TILE
    n_ctx = tc // ROW_TILE if with_ctx else 0
    n_rows = tl + (tc if with_ctx else 0)

    def mod_row(bi, t):
        return jnp.where(t < n_lat, bi, 2) * N_MOD

    return pl.pallas_call(
        functools.partial(_adaln_kernel, n_lat=n_lat),
        grid=(b, n_lat + n_ctx),
        in_specs=[
            pl.BlockSpec((None, ROW_TILE, d), lambda bi, t: (bi, jnp.minimum(t, n_lat - 1), 0)),
            pl.BlockSpec((None, ROW_TILE, d), lambda bi, t: (bi, jnp.maximum(t - n_lat, 0), 0)),
            pl.BlockSpec((1, d), lambda bi, t: (0, 0)),
            pl.BlockSpec((1, 1, d), lambda bi, t: (mod_row(bi, t) + idx_shift, 0, 0)),
            pl.BlockSpec((1, 1, d), lambda bi, t: (mod_row(bi, t) + idx_scale, 0, 0)),
        ],
        out_specs=pl.BlockSpec((None, ROW_TILE, d), lambda bi, t: (bi, t, 0)),
        out_shape=jax.ShapeDtypeStruct((b, n_rows, d), BF16),
        compiler_params=_params(2),
        name="adaln_norm",
    )(x_lat, x_ctx, g.reshape(1, d), mods, mods)


def _final_norm_kernel(x_ref, g_ref, o_ref):
    o_ref[...] = _rms(x_ref[...], g_ref[...])


def _final_norm(x, g):
    b, t, d = x.shape
    return pl.pallas_call(
        _final_norm_kernel,
        grid=(b, t // ROW_TILE),
        in_specs=[pl.BlockSpec((None, ROW_TILE, d), lambda bi, i: (bi, i, 0)),
                  pl.BlockSpec((1, d), lambda bi, i: (0, 0))],
        out_specs=pl.BlockSpec((None, ROW_TILE, d), lambda bi, i: (bi, i, 0)),
        out_shape=jax.ShapeDtypeStruct((b, t, d), F32),
        compiler_params=_params(2),
        name="final_norm",
    )(x, g.reshape(1, d))


def _mm_kernel(a_ref, b_ref, o_ref):
    o_ref[...] = jnp.dot(a_ref[...], b_ref[...], preferred_element_type=F32).astype(o_ref.dtype)


def _matmul(a, w, tm_pref=1536, tn_pref=512):
    m, k = a.shape
    n = w.shape[1]
    tm = _tile(m, tm_pref, ROW_TILE)
    tn = _tile(n, tn_pref, LANES)
    return pl.pallas_call(
        _mm_kernel,
        grid=(m // tm, n // tn),
        in_specs=[pl.BlockSpec((tm, k), lambda i, j: (i, 0)),
                  pl.BlockSpec((k, tn), lambda i, j: (0, j))],
        out_specs=pl.BlockSpec((tm, tn), lambda i, j: (i, j)),
        out_shape=jax.ShapeDtypeStruct((m, n), BF16),
        compiler_params=_params(2),
        name="matmul",
    )(a, w)


def _mm_nt_kernel(a_ref, wt_ref, o_ref):
    o_ref[...] = lax.dot_general(a_ref[...], wt_ref[...], (((1,), (1,)), ((), ())),
                                 preferred_element_type=F32).astype(o_ref.dtype)


def _matmul_nt(a, wt, tm_pref=1536, tn_pref=512):
    m, k = a.shape
    n = wt.shape[0]
    tm = _tile(m, tm_pref, ROW_TILE)
    tn = _tile(n, tn_pref, LANES)
    return pl.pallas_call(
        _mm_nt_kernel,
        grid=(m // tm, n // tn),
        in_specs=[pl.BlockSpec((tm, k), lambda i, j: (i, 0)),
                  pl.BlockSpec((tn, k), lambda i, j: (j, 0))],
        out_specs=pl.BlockSpec((tm, tn), lambda i, j: (i, j)),
        out_shape=jax.ShapeDtypeStruct((m, n), BF16),
        compiler_params=_params(2),
        name="matmul_nt",
    )(a, wt)


def _swiglu_kernel(a_ref, wg_ref, wu_ref, o_ref):
    a = a_ref[...]
    g = jnp.dot(a, wg_ref[...], preferred_element_type=F32)
    u = jnp.dot(a, wu_ref[...], preferred_element_type=F32)
    o_ref[...] = (g * jax.nn.sigmoid(g) * u).astype(o_ref.dtype)


def _swiglu_in(h, w_in, tm_pref=1536, tn_pref=512):
    m, k = h.shape
    f = w_in.shape[1] // 2
    tm = _tile(m, tm_pref, ROW_TILE)
    tn = _tile(f, tn_pref, LANES)
    nj = f // tn
    return pl.pallas_call(
        _swiglu_kernel,
        grid=(m // tm, nj),
        in_specs=[pl.BlockSpec((tm, k), lambda i, j: (i, 0)),
                  pl.BlockSpec((k, tn), lambda i, j: (0, j)),
                  pl.BlockSpec((k, tn), lambda i, j: (0, j + nj))],
        out_specs=pl.BlockSpec((tm, tn), lambda i, j: (i, j)),
        out_shape=jax.ShapeDtypeStruct((m, f), BF16),
        compiler_params=_params(2),
        name="swiglu_in",
    )(h, w_in, w_in)


def _residual_kernel(a_ref, w_ref, x_ref, gate_ref, o_ref, *, coef):
    y = jnp.dot(a_ref[...], w_ref[...], preferred_element_type=F32)
    gate = gate_ref[0] if coef is None else coef * gate_ref[0]
    o_ref[...] = x_ref[...] + gate * y


def _residual_matmul(a, w, x, mods, idx_gate, coef, a_row_off, is_ctx, tm_pref=1024, tn_pref=512):
    b, r, n = x.shape
    k = a.shape[2]
    tm = _tile(r, tm_pref, ROW_TILE)
    tn = _tile(n, tn_pref, LANES)
    off = a_row_off // tm

    def gate_row(bi):
        return (2 if is_ctx else bi) * N_MOD + idx_gate

    return pl.pallas_call(
        functools.partial(_residual_kernel, coef=coef),
        grid=(b, r // tm, n // tn),
        in_specs=[pl.BlockSpec((None, tm, k), lambda bi, i, j: (bi, i + off, 0)),
                  pl.BlockSpec((k, tn), lambda bi, i, j: (0, j)),
                  pl.BlockSpec((None, tm, tn), lambda bi, i, j: (bi, i, j)),
                  pl.BlockSpec((1, 1, tn), lambda bi, i, j: (gate_row(bi), 0, j))],
        out_specs=pl.BlockSpec((None, tm, tn), lambda bi, i, j: (bi, i, j)),
        out_shape=jax.ShapeDtypeStruct((b, r, n), F32),
        compiler_params=_params(3),
        name="residual_matmul",
    )(a, w, x, mods)


def _merge_kernel(a0_ref, a1_ref, a2_ref, w0_ref, w1_ref, w2_ref, g0_ref, g1_ref, g2_ref, o_ref):
    def branch(a_ref, w_ref, g_ref):
        y = jnp.dot(a_ref[...], w_ref[...], preferred_element_type=F32)
        return jax.nn.sigmoid(g_ref[...].astype(F32)) * y

    acc = branch(a0_ref, w0_ref, g0_ref) + branch(a1_ref, w1_ref, g1_ref)
    o_ref[...] = (acc + branch(a2_ref, w2_ref, g2_ref)).astype(o_ref.dtype)


def _merge(o_gqa, o_mla, y_lru, proj, w0, w1, w2, gate_col, rows, mix_row_off, tm_pref=1024, tn_pref=512):
    b = o_gqa.shape[0]
    n = w0.shape[1]
    tm = _tile(rows, tm_pref, ROW_TILE)
    tn = _tile(n, tn_pref, LANES)
    off = mix_row_off // tm
    gcol = gate_col // tn
    nj = n // tn

    def a_spec(a, row_off):
        return pl.BlockSpec((None, tm, a.shape[2]), lambda bi, i, j: (bi, i + row_off, 0))

    def w_spec(w):
        return pl.BlockSpec((w.shape[0], tn), lambda bi, i, j: (0, j))

    def gate_spec(br):
        return pl.BlockSpec((None, tm, tn), lambda bi, i, j: (bi, i + off, gcol + br * nj + j))

    return pl.pallas_call(
        _merge_kernel,
        grid=(b, rows // tm, nj),
        in_specs=[a_spec(o_gqa, 0), a_spec(o_mla, 0), a_spec(y_lru, off), w_spec(w0), w_spec(w1), w_spec(w2),
                  gate_spec(0), gate_spec(1), gate_spec(2)],
        out_specs=pl.BlockSpec((None, tm, tn), lambda bi, i, j: (bi, i, j)),
        out_shape=jax.ShapeDtypeStruct((b, rows, n), BF16),
        compiler_params=_params(3),
        name="merge",
    )(o_gqa, o_mla, y_lru, w0, w1, w2, proj, proj, proj)


def _rope_tables(n_tok, rot_dim, grid_w, ctx_len):
    rows = n_tok // grid_w
    r_idx, c_idx = jnp.meshgrid(jnp.arange(rows), jnp.arange(grid_w), indexing='ij')
    r_idx = r_idx.reshape(-1).astype(F32)
    c_idx = c_idx.reshape(-1).astype(F32)
    n_pairs = rot_dim // 4
    freqs = jnp.power(ROPE_THETA, -jnp.arange(n_pairs, dtype=F32) / n_pairs)
    ang = jnp.concatenate([r_idx[:, None] * freqs, c_idx[:, None] * freqs], axis=-1)
    cos, sin = jnp.cos(ang), jnp.sin(ang)
    cos_full = jnp.concatenate([cos, cos], axis=-1)
    sin_signed = jnp.concatenate([-sin, sin], axis=-1)
    reps = LANES // rot_dim
    cos_full = jnp.tile(cos_full, (1, reps))
    sin_signed = jnp.tile(sin_signed, (1, reps))
    cos_full = jnp.concatenate([cos_full, jnp.ones((ctx_len, LANES), F32)], axis=0)
    sin_signed = jnp.concatenate([sin_signed, jnp.zeros((ctx_len, LANES), F32)], axis=0)
    return cos_full, sin_signed


def _rope128(x, cos, sin):
    return x * cos + pltpu.roll(x, LANES // 2, 1) * sin


def _rope64(x, cos, sin):
    lane = lax.broadcasted_iota(jnp.int32, x.shape, 1)
    first_half = (lane % 64) < 32
    partner = jnp.where(first_half, pltpu.roll(x, LANES - 32, 1), pltpu.roll(x, 32, 1))
    return x * cos + partner * sin


def _prep_kernel(k_ref, cq_ref, ckv_ref, kr_ref, cg_ref, sg_ref, cm_ref, sm_ref,
                 gk_ref, gq_ref, gkv_ref, ko_ref, cqo_ref, ckvo_ref, kro_ref, *, kv_heads):
    cos_g, sin_g = cg_ref[...], sg_ref[...]
    for h in range(kv_heads):
        cols = slice(h * LANES, (h + 1) * LANES)
        y = _rms(k_ref[:, cols].astype(F32), gk_ref[...])
        ko_ref[:, cols] = _rope128(y, cos_g, sin_g).astype(BF16)
    cqo_ref[...] = _rms(cq_ref[...].astype(F32), gq_ref[...]).astype(BF16)
    ckvo_ref[...] = _rms(ckv_ref[...].astype(F32), gkv_ref[...]).astype(BF16)
    kro_ref[...] = _rope64(kr_ref[...].astype(F32), cm_ref[...], sm_ref[...]).astype(BF16)


def _prep(proj, lay, dm, tabs_g, tabs_m, gk, gq, gkv):
    b, tt, _ = proj.shape
    kw = dm.gqa_kv * LANES
    assert lay.k % kw == 0 and lay.cq % dm.q_rank == 0 and lay.ckv % dm.kv_rank == 0
    row = lambda bi, t: (bi, t, 0)
    tab = pl.BlockSpec((ROW_TILE, LANES), lambda bi, t: (t, 0))

    def vec(n):
        return pl.BlockSpec((1, n), lambda bi, t: (0, 0))

    return pl.pallas_call(
        functools.partial(_prep_kernel, kv_heads=dm.gqa_kv),
        grid=(b, tt // ROW_TILE),
        in_specs=[pl.BlockSpec((None, ROW_TILE, kw), lambda bi, t: (bi, t, lay.k // kw)),
                  pl.BlockSpec((None, ROW_TILE, dm.q_rank), lambda bi, t: (bi, t, lay.cq // dm.q_rank)),
                  pl.BlockSpec((None, ROW_TILE, dm.kv_rank), lambda bi, t: (bi, t, lay.ckv // dm.kv_rank)),
                  pl.BlockSpec((None, ROW_TILE, LANES), lambda bi, t: (bi, t, lay.krope // LANES)),
                  tab, tab, tab, tab, vec(LANES), vec(dm.q_rank), vec(dm.kv_rank)],
        out_specs=[pl.BlockSpec((None, ROW_TILE, kw), row),
                   pl.BlockSpec((None, ROW_TILE, dm.q_rank), row),
                   pl.BlockSpec((None, ROW_TILE, dm.kv_rank), row),
                   pl.BlockSpec((None, ROW_TILE, LANES), row)],
        out_shape=[jax.ShapeDtypeStruct((b, tt, kw), BF16),
                   jax.ShapeDtypeStruct((b, tt, dm.q_rank), BF16),
                   jax.ShapeDtypeStruct((b, tt, dm.kv_rank), BF16),
                   jax.ShapeDtypeStruct((b, tt, LANES), BF16)],
        compiler_params=_params(2),
        name="prep",
    )(proj, proj, proj, proj, tabs_g[0], tabs_g[1], tabs_m[0], tabs_m[1],
      gk.reshape(1, LANES), gq.reshape(1, dm.q_rank), gkv.reshape(1, dm.kv_rank))


LOG2E = 1.4426950408889634
KEY_CHUNK = 768


SCORE_BUFFERS = 4
ONES_ROWS = 16


def _flash_scratch(dk, n_q, key_rows, tk):
    return ([pltpu.VMEM((dk, n_q), BF16),
             pltpu.VMEM((key_rows // tk, LANES + ONES_ROWS, tk), BF16),
             pltpu.VMEM((1, n_q), F32),
             pltpu.VMEM((LANES + ONES_ROWS, n_q), F32)]
            + [pltpu.VMEM((tk, n_q), F32)] * SCORE_BUFFERS
            + [pltpu.VMEM((1, n_q), F32)] * SCORE_BUFFERS)


def _flash_t(qt_ref, load_k, load_vt, n_chunks, stats):
    m_ref, acc_ref = stats[:2]
    s_refs = stats[2:2 + SCORE_BUFFERS]
    smax_refs = stats[2 + SCORE_BUFFERS:]
    m_ref[...] = jnp.full(m_ref.shape, -jnp.inf, F32)
    acc_ref[...] = jnp.zeros(acc_ref.shape, F32)

    def scores(c, slot):
        s = jnp.dot(load_k(c), qt_ref[...], preferred_element_type=F32)
        s_refs[slot][...] = s
        smax_refs[slot][...] = jnp.max(s, axis=0, keepdims=True)

    def update(c, slot):
        m_prev = m_ref[...]
        m_new = jnp.maximum(m_prev, smax_refs[slot][...])
        p = jnp.exp2(s_refs[slot][...] - m_new)
        alpha = jnp.exp2(m_prev - m_new)
        acc_ref[...] = alpha * acc_ref[...] + jnp.dot(load_vt(c), p.astype(BF16),
                                                     preferred_element_type=F32)
        m_ref[...] = m_new

    scores(0, 0)
    n_main = (n_chunks - 1) // SCORE_BUFFERS

    def body(i, carry):
        c = SCORE_BUFFERS * i
        for j in range(SCORE_BUFFERS):
            scores(c + j + 1, (j + 1) % SCORE_BUFFERS)
            update(c + j, j)
        return carry

    lax.fori_loop(0, n_main, body, 0)
    for c in range(SCORE_BUFFERS * n_main, n_chunks):
        if c + 1 < n_chunks:
            scores(c + 1, (c + 1) % SCORE_BUFFERS)
        update(c, c % SCORE_BUFFERS)
    return acc_ref[:LANES, :] / acc_ref[LANES:LANES + 1, :]


def _stage_values_t(v_ref, vt_ref, tk):
    @pl.when(pl.program_id(2) == 0)
    def _():
        for c in range(vt_ref.shape[0]):
            vt_ref[c, :LANES, :] = v_ref[c * tk:(c + 1) * tk, :].astype(F32).T.astype(BF16)
            vt_ref[c, LANES:, :] = jnp.ones((ONES_ROWS, tk), BF16)


def _gqa_kernel(q_ref, k_ref, v_ref, cos_ref, sin_ref, gq_ref, o_ref, qt_ref, vt_ref, *stats,
                group, tq, tk, scale):
    _stage_values_t(v_ref, vt_ref, tk)
    cos, sin = cos_ref[...], sin_ref[...]
    for g in range(group):
        y = _rms(q_ref[:, g * LANES:(g + 1) * LANES].astype(F32), gq_ref[...])
        qt_ref[:, g * tq:(g + 1) * tq] = (_rope128(y, cos, sin) * scale).T.astype(BF16)
    out_t = _flash_t(qt_ref,
                     lambda c: k_ref[pl.ds(pl.multiple_of(c * tk, tk), tk), :],
                     lambda c: vt_ref[c],
                     vt_ref.shape[0], stats)
    for g in range(group):
        o_ref[:, g * LANES:(g + 1) * LANES] = out_t[:, g * tq:(g + 1) * tq].T.astype(o_ref.dtype)


def _gqa_attention(proj, k_prep, lay, dm, tabs, gq, q_rows, q_off, key_rows, key_off, tq_pref=512):
    b = proj.shape[0]
    group = dm.gqa_heads // dm.gqa_kv
    gw = group * LANES
    tq = _tile(q_rows, tq_pref, ROW_TILE)
    tk = _tile(key_rows, KEY_CHUNK, ROW_TILE)
    assert q_off % tq == 0 and key_off % key_rows == 0
    qo, ko, vcol = q_off // tq, key_off // key_rows, lay.v // LANES
    return pl.pallas_call(
        functools.partial(_gqa_kernel, group=group, tq=tq, tk=tk, scale=float(LANES) ** -0.5 * LOG2E),
        grid=(b, dm.gqa_kv, q_rows // tq),
        in_specs=[pl.BlockSpec((None, tq, gw), lambda bi, h, i: (bi, i + qo, h)),
                  pl.BlockSpec((None, key_rows, LANES), lambda bi, h, i: (bi, ko, h)),
                  pl.BlockSpec((None, key_rows, LANES), lambda bi, h, i: (bi, ko, vcol + h)),
                  pl.BlockSpec((tq, LANES), lambda bi, h, i: (i + qo, 0)),
                  pl.BlockSpec((tq, LANES), lambda bi, h, i: (i + qo, 0)),
                  pl.BlockSpec((1, LANES), lambda bi, h, i: (0, 0))],
        out_specs=pl.BlockSpec((None, tq, gw), lambda bi, h, i: (bi, i, h)),
        out_shape=jax.ShapeDtypeStruct((b, q_rows, dm.gqa_heads * LANES), BF16),
        scratch_shapes=_flash_scratch(LANES, group * tq, key_rows, tk),
        compiler_params=_params(3),
        name="gqa_attention",
    )(proj, k_prep, proj, tabs[0], tabs[1], gq.reshape(1, LANES))


def _mla_kernel(qn_ref, qr_ref, kn_ref, kr_ref, v_ref, cos_ref, sin_ref, o_ref,
                qt_ref, vt_ref, *stats, tk, scale):
    _stage_values_t(v_ref, vt_ref, tk)
    qt_ref[:LANES, :] = (qn_ref[...].astype(F32) * scale).T.astype(BF16)
    qr = _rope64(qr_ref[...].astype(F32), cos_ref[...], sin_ref[...])
    qt_ref[LANES:, :] = (qr * scale).T.astype(BF16)

    def load_k(c):
        rows = pl.ds(pl.multiple_of(c * tk, tk), tk)
        return jnp.concatenate([kn_ref[rows, :], kr_ref[rows, :]], axis=1)

    out_t = _flash_t(qt_ref, load_k, lambda c: vt_ref[c], vt_ref.shape[0], stats)
    o_ref[...] = out_t.T.astype(o_ref.dtype)


def _mla_attention(q_up, kv_up, k_rope, dm, tabs, q_rows, q_off, key_rows, key_off, tq_pref=2048):
    b = q_up.shape[0]
    h_n = dm.mla_heads
    tq = _tile(q_rows, tq_pref, ROW_TILE)
    tk = _tile(key_rows, KEY_CHUNK, ROW_TILE)
    assert q_off % tq == 0 and key_off % key_rows == 0
    qo, ko = q_off // tq, key_off // key_rows
    scale = float(LANES + dm.rope_dim) ** -0.5 * LOG2E
    return pl.pallas_call(
        functools.partial(_mla_kernel, tk=tk, scale=scale),
        grid=(b, h_n, q_rows // tq),
        in_specs=[pl.BlockSpec((None, tq, LANES), lambda bi, h, i: (bi, i + qo, h)),
                  pl.BlockSpec((None, tq, LANES), lambda bi, h, i: (bi, i + qo, h_n + h)),
                  pl.BlockSpec((None, key_rows, LANES), lambda bi, h, i: (bi, ko, h)),
                  pl.BlockSpec((None, key_rows, LANES), lambda bi, h, i: (bi, ko, 0)),
                  pl.BlockSpec((None, key_rows, LANES), lambda bi, h, i: (bi, ko, h_n + h)),
                  pl.BlockSpec((tq, LANES), lambda bi, h, i: (i + qo, 0)),
                  pl.BlockSpec((tq, LANES), lambda bi, h, i: (i + qo, 0))],
        out_specs=pl.BlockSpec((None, tq, LANES), lambda bi, h, i: (bi, i, h)),
        out_shape=jax.ShapeDtypeStruct((b, q_rows, h_n * LANES), BF16),
        scratch_shapes=_flash_scratch(2 * LANES, tq, key_rows, tk),
        compiler_params=_params(3),
        name="mla_attention",
    )(q_up, q_up, kv_up, k_rope, kv_up, tabs[0], tabs[1])


def _lru_kernel(lx_ref, lg_ref, cw_ref, cb_ref, wa_ref, ba_ref, wi_ref, bi_ref, lam_ref, o_ref,
                xs_ref, xc_ref, hf_ref, hb_ref, *, tl, tc, ch):
    tt = tl + tc
    n_lat, n_ctx = tl // ch, tc // ch
    pad = SUBLANES
    zeros = jnp.zeros((pad, LANES), F32)
    xs_ref[0:pad, :] = zeros
    xs_ref[pad + tt:2 * pad + tt, :] = zeros

    def stage(c, carry):
        r = pl.multiple_of(c * ch, ch)
        xs_ref[pl.ds(pad + r, ch), :] = lx_ref[pl.ds(r, ch), :].astype(F32)
        return carry

    lax.fori_loop(0, n_lat + n_ctx, stage, 0)

    def conv(c, carry):
        r = pl.multiple_of(c * ch, ch)
        ext = xs_ref[pl.ds(r, ch + 2 * pad), :]
        n_ext = ch + 2 * pad
        starts = jnp.logical_or(c == 0, c == n_lat)
        ends = jnp.logical_or(c == n_lat - 1, c == n_lat + n_ctx - 1)
        head = jnp.where(starts, 0.0, ext[:pad])
        tail = jnp.where(ends, 0.0, ext[pad + ch:])
        ext = jnp.concatenate([head, ext[pad:pad + ch], tail], axis=0)
        y = cb_ref[...]
        for j in range(CONV_WIDTH):
            shift = j - 1
            tap = ext if shift == 0 else pltpu.roll(ext, (-shift) % n_ext, 0)
            y = y + tap[pad:pad + ch] * cw_ref[j:j + 1, :]
        xc_ref[pl.ds(r, ch), :] = y
        return carry

    lax.fori_loop(0, n_lat + n_ctx, conv, 0)

    row = lax.broadcasted_iota(jnp.int32, (SUBLANES, LANES), 0)

    def coeffs(xc, d):
        xb = xc.astype(BF16)
        r = jax.nn.sigmoid(jnp.dot(xb, wa_ref[d].astype(BF16), preferred_element_type=F32) + ba_ref[d:d + 1, :])
        i = jax.nn.sigmoid(jnp.dot(xb, wi_ref[d].astype(BF16), preferred_element_type=F32) + bi_ref[d:d + 1, :])
        z = -lam_ref[d:d + 1, :]
        softplus = jnp.maximum(z, 0.0) + jnp.log1p(jnp.exp(-jnp.abs(z)))
        log_a = -LRU_C * r * softplus
        a = jnp.exp(log_a)
        u = jnp.sqrt(1.0 - a * a) * (i * xc)
        return a, u

    def scan_chunk(c, h, d, out_ref):
        r = pl.multiple_of(c * ch, ch)
        a, u = coeffs(xc_ref[pl.ds(r, ch), :], d)
        n_blk = ch // SUBLANES
        order = range(n_blk) if d == 0 else range(n_blk - 1, -1, -1)
        edge = SUBLANES - 1 if d == 0 else 0
        blocks = []
        for j in range(n_blk):
            aj = a[j * SUBLANES:(j + 1) * SUBLANES]
            uj = u[j * SUBLANES:(j + 1) * SUBLANES]
            for dist in (1, 2, 4):
                if d == 0:
                    shift, mask = dist, row >= dist
                else:
                    shift, mask = SUBLANES - dist, row < SUBLANES - dist
                a_s = jnp.where(mask, pltpu.roll(aj, shift, 0), 1.0)
                u_s = jnp.where(mask, pltpu.roll(uj, shift, 0), 0.0)
                uj = uj + aj * u_s
                aj = aj * a_s
            blocks.append((aj, uj))
        carry_in = [None] * n_blk
        for j in order:
            carry_in[j] = h
            aj, uj = blocks[j]
            h = aj[edge:edge + 1] * h + uj[edge:edge + 1]
        states = [blocks[j][0] * carry_in[j] + blocks[j][1] for j in range(n_blk)]
        out_ref[pl.ds(r, ch), :] = jnp.concatenate(states, axis=0)
        return h

    def both(fwd_chunk, bwd_chunk, carry):
        return (scan_chunk(fwd_chunk, carry[0], 0, hf_ref), scan_chunk(bwd_chunk, carry[1], 1, hb_ref))

    h0 = jnp.zeros((1, LANES), F32)
    carry = lax.fori_loop(0, n_ctx, lambda k, cr: both(n_lat + k, n_lat + n_ctx - 1 - k, cr), (h0, h0))
    lax.fori_loop(0, n_lat, lambda k, cr: both(k, n_lat - 1 - k, cr), carry)

    def combine(c, carry):
        rows = pl.ds(pl.multiple_of(c * ch, ch), ch)
        gate = jax.nn.gelu(lg_ref[rows, :].astype(F32))
        o_ref[rows, :] = ((hf_ref[rows, :] + hb_ref[rows, :]) * gate).astype(o_ref.dtype)
        return carry

    lax.fori_loop(0, n_lat + n_ctx, combine, 0)


def _lru(proj, lay, dm, tl, conv_w, conv_b, wa, ba, wi, bi, lam):
    b, tt, _ = proj.shape
    nb = dm.lru_w // LANES
    xcol, gcol = lay.lx // LANES, lay.lg // LANES
    vec2 = pl.BlockSpec((2, LANES), lambda bi_, n: (0, n))
    wspec = pl.BlockSpec((2, None, LANES, LANES), lambda bi_, n: (0, n, 0, 0))
    return pl.pallas_call(
        functools.partial(_lru_kernel, tl=tl, tc=tt - tl, ch=ROW_TILE),
        grid=(b, nb),
        in_specs=[pl.BlockSpec((None, tt, LANES), lambda bi_, n: (bi_, 0, xcol + n)),
                  pl.BlockSpec((None, tt, LANES), lambda bi_, n: (bi_, 0, gcol + n)),
                  pl.BlockSpec((CONV_WIDTH, LANES), lambda bi_, n: (0, n)),
                  pl.BlockSpec((1, LANES), lambda bi_, n: (0, n)),
                  wspec, vec2, wspec, vec2, vec2],
        out_specs=pl.BlockSpec((None, tt, LANES), lambda bi_, n: (bi_, 0, n)),
        out_shape=jax.ShapeDtypeStruct((b, tt, dm.lru_w), BF16),
        scratch_shapes=[pltpu.VMEM((tt + 2 * SUBLANES, LANES), F32),
                        pltpu.VMEM((tt, LANES), F32),
                        pltpu.VMEM((tt, LANES), F32),
                        pltpu.VMEM((tt, LANES), F32)],
        compiler_params=_params(2),
        name="rglru",
    )(proj, proj, conv_w, conv_b.reshape(1, dm.lru_w), wa, ba, wi, bi, lam)


def _cast_kernel(w_ref, o_ref):
    o_ref[...] = w_ref[...].astype(o_ref.dtype)


def _cast_layer(w, l):
    _, r, c = w.shape
    tr = _tile(r, 512, SUBLANES)
    tc = _tile(c, 2048, LANES)
    return pl.pallas_call(
        _cast_kernel,
        grid=(r // tr, c // tc),
        in_specs=[pl.BlockSpec((None, tr, tc), lambda i, j: (l, i, j))],
        out_specs=pl.BlockSpec((tr, tc), lambda i, j: (i, j)),
        out_shape=jax.ShapeDtypeStruct((r, c), BF16),
        compiler_params=_params(2),
        name="cast_weight",
    )(w)


def _proj_weight_kernel(a_ref, b_ref, o_ref, *, n_copy, n_lx, shift):
    j = pl.program_id(0)

    @pl.when(j < n_copy)
    def _():
        o_ref[...] = b_ref[...].astype(o_ref.dtype)

    @pl.when(j == n_copy)
    def _():
        row = lax.broadcasted_iota(jnp.int32, b_ref.shape, 0)
        o_ref[...] = jnp.where(row < shift, b_ref[...], 0.0).astype(o_ref.dtype)

    @pl.when(jnp.logical_and(j > n_copy, j < n_lx))
    def _():
        o_ref[...] = jnp.zeros(o_ref.shape, o_ref.dtype)

    @pl.when(j >= n_lx)
    def _():
        o_ref[...] = jnp.concatenate([a_ref[shift:, :], b_ref[:shift, :]], axis=0).astype(o_ref.dtype)


def _proj_weight_t(w_in_t, l, lay, dm):
    _, n_in, d = w_in_t.shape
    real_attn = lay.krope + dm.rope_dim
    bw = 512
    while lay.krope % bw or lay.lx % bw:
        bw //= 2
    shift = real_attn - lay.krope
    assert bw >= SUBLANES and shift < bw and shift % SUBLANES == 0
    n_copy = lay.krope // bw
    n_lx = lay.lx // bw
    last_in = (n_in - 1) // bw

    def a_idx(j):
        return jnp.clip(n_copy + j - n_lx, 0, last_in)

    def b_idx(j):
        return jnp.minimum(jnp.where(j < n_lx, j, n_copy + j - n_lx + 1), last_in)

    return pl.pallas_call(
        functools.partial(_proj_weight_kernel, n_copy=n_copy, n_lx=n_lx, shift=shift),
        grid=(lay.total // bw,),
        in_specs=[pl.BlockSpec((None, bw, d), lambda j: (l, a_idx(j), 0)),
                  pl.BlockSpec((None, bw, d), lambda j: (l, b_idx(j), 0))],
        out_specs=pl.BlockSpec((bw, d), lambda j: (j, 0)),
        out_shape=jax.ShapeDtypeStruct((lay.total, d), BF16),
        compiler_params=_params(1),
        name="proj_weight",
    )(w_in_t, w_in_t)


def _layer_weights(l, dm, lay, w_ff1_in, w_ff1_out, w_ff2_in, w_ff2_out, w_in, w_mla_uq, w_mla_ukv,
                   w_o_gqa, w_o_mla, w_o_lru, w_out):
    w_proj = _proj_weight_t(jnp.swapaxes(w_in, 1, 2), l, lay, dm)
    h = dm.mla_heads
    uq = w_mla_uq[l].reshape(dm.q_rank, h, LANES + dm.rope_dim)
    uq_rope = jnp.pad(uq[:, :, LANES:], ((0, 0), (0, 0), (0, LANES - dm.rope_dim)))
    w_uq = jnp.concatenate([uq[:, :, :LANES].reshape(dm.q_rank, h * LANES),
                            uq_rope.reshape(dm.q_rank, h * LANES)], axis=1).astype(BF16)
    ukv = w_mla_ukv[l].reshape(dm.kv_rank, h, 2 * LANES)
    w_ukv = jnp.concatenate([ukv[:, :, :LANES].reshape(dm.kv_rank, h * LANES),
                             ukv[:, :, LANES:].reshape(dm.kv_rank, h * LANES)], axis=1).astype(BF16)
    cast = functools.partial(_cast_layer, l=l)
    return dict(ff1_in=cast(w_ff1_in), ff1_out=cast(w_ff1_out), ff2_in=cast(w_ff2_in), ff2_out=cast(w_ff2_out),
                proj=w_proj, uq=w_uq, ukv=w_ukv, o_gqa=cast(w_o_gqa), o_mla=cast(w_o_mla),
                o_lru=cast(w_o_lru), out=cast(w_out))


def _forward(dm, x, c, ctx, c_ctx, w_mod, b_mod, g_norm, w_ff1_in, w_ff1_out, w_ff2_in, w_ff2_out, w_in,
             gqa_q_norm, gqa_k_norm, mla_q_norm, mla_kv_norm, w_mla_uq, w_mla_ukv, lru_conv_w, lru_conv_b,
             lru_wa, lru_ba, lru_wi, lru_bi, lru_lambda, w_o_gqa, w_o_mla, w_o_lru, w_out, final_norm):
    b, tl, d = x.shape
    tc = ctx.shape[1]
    tt = tl + tc
    depth = w_mod.shape[0]
    assert b == 2 and tl % ROW_TILE == 0 and tc % ROW_TILE == 0
    lay = _proj_layout(dm, d)
    tabs_g = _rope_tables(tl, LANES, dm.grid_w, tc)
    tabs_m = _rope_tables(tl, dm.rope_dim, dm.grid_w, tc)

    c_rows = jnp.concatenate([c, c_ctx[None, :], jnp.zeros((SUBLANES - b - 1, d), F32)], axis=0)
    mods_all = _mod_vectors(c_rows, w_mod, b_mod)

    xc = ctx
    for l in range(depth):
        last = l == depth - 1
        w = _layer_weights(l, dm, lay, w_ff1_in, w_ff1_out, w_ff2_in, w_ff2_out, w_in, w_mla_uq, w_mla_ukv,
                           w_o_gqa, w_o_mla, w_o_lru, w_out)
        mods = mods_all[l, :b + 1].reshape((b + 1) * N_MOD, 1, d)

        h1 = _adaln_norm(x, xc, g_norm[l, 0], mods, 0, 1, True)
        act = _swiglu_in(h1.reshape(b * tt, d), w['ff1_in']).reshape(b, tt, dm.d_ff)
        x = _residual_matmul(act, w['ff1_out'], x, mods, 2, 0.5, 0, False)
        xc = _residual_matmul(act, w['ff1_out'], xc, mods, 2, 0.5, tl, True)

        h2 = _adaln_norm(x, xc, g_norm[l, 1], mods, 3, 4, True)
        proj = _matmul_nt(h2.reshape(b * tt, d), w['proj']).reshape(b, tt, lay.total)
        k_gqa, cq_n, ckv_n, k_rope = _prep(proj, lay, dm, tabs_g, tabs_m, gqa_k_norm[l], mla_q_norm[l],
                                           mla_kv_norm[l])
        q_up = _matmul(cq_n.reshape(b * tt, dm.q_rank), w['uq']).reshape(b, tt, -1)
        kv_up = _matmul(ckv_n.reshape(b * tt, dm.kv_rank), w['ukv']).reshape(b, tt, -1)
        y_lru = _lru(proj, lay, dm, tl, lru_conv_w[l], lru_conv_b[l], lru_wa[l], lru_ba[l], lru_wi[l],
                     lru_bi[l], lru_lambda[l])

        o_gqa = _gqa_attention(proj, k_gqa, lay, dm, tabs_g, gqa_q_norm[l], tl, 0, tt, 0)
        o_mla = _mla_attention(q_up, kv_up, k_rope, dm, tabs_m, tl, 0, tt, 0)
        merged = _merge(o_gqa, o_mla, y_lru, proj, w['o_gqa'], w['o_mla'], w['o_lru'], lay.gate, tl, 0)
        x = _residual_matmul(merged, w['out'], x, mods, 5, None, 0, False)
        if not last:
            o_gqa_c = _gqa_attention(proj, k_gqa, lay, dm, tabs_g, gqa_q_norm[l], tc, tl, tc, tl)
            o_mla_c = _mla_attention(q_up, kv_up, k_rope, dm, tabs_m, tc, tl, tc, tl)
            merged_c = _merge(o_gqa_c, o_mla_c, y_lru, proj, w['o_gqa'], w['o_mla'], w['o_lru'],
                              lay.gate, tc, tl)
            xc = _residual_matmul(merged_c, w['out'], xc, mods, 5, None, 0, True)

        h3 = _adaln_norm(x, xc, g_norm[l, 2], mods, 6, 7, not last)
        rows = h3.shape[1]
        act = _swiglu_in(h3.reshape(b * rows, d), w['ff2_in']).reshape(b, rows, dm.d_ff)
        x = _residual_matmul(act, w['ff2_out'], x, mods, 8, 0.5, 0, False)
        if not last:
            xc = _residual_matmul(act, w['ff2_out'], xc, mods, 8, 0.5, tl, True)
    return _final_norm(x, final_norm)


def kernel(x, c, ctx, c_ctx, w_mod, b_mod, g_norm, w_ff1_in, w_ff1_out, w_ff2_in, w_ff2_out, w_in, gqa_q_norm, gqa_k_norm, mla_q_norm, mla_kv_norm, w_mla_uq, w_mla_ukv, lru_conv_w, lru_conv_b, lru_wa, lru_ba, lru_wi, lru_bi, lru_lambda, w_o_gqa, w_o_mla, w_o_lru, w_out, final_norm):
    return _forward(DIMS, x, c, ctx, c_ctx, w_mod, b_mod, g_norm, w_ff1_in, w_ff1_out, w_ff2_in, w_ff2_out,
                    w_in, gqa_q_norm, gqa_k_norm, mla_q_norm, mla_kv_norm, w_mla_uq, w_mla_ukv, lru_conv_w,
                    lru_conv_b, lru_wa, lru_ba, lru_wi, lru_bi, lru_lambda, w_o_gqa, w_o_mla, w_o_lru, w_out,
                    final_norm)
```

```python
import dataclasses
import functools

import jax
import jax.numpy as jnp
from jax import lax
from jax.experimental import pallas as pl
from jax.experimental.pallas import tpu as pltpu

F32 = jnp.float32
BF16 = jnp.bfloat16

LANES = 128
SUBLANES = 8
VMEM_LIMIT = 60 * 1024 * 1024

ROPE_THETA = 10000.0
NORM_EPS = 1e-6
N_MOD = 9
LRU_C = 8.0
CONV_WIDTH = 4
ROW_TILE = 256


@dataclasses.dataclass(frozen=True)
class Dims:
    grid_w: int = 64
    gqa_heads: int = 16
    gqa_kv: int = 4
    mla_heads: int = 16
    q_rank: int = 1536
    kv_rank: int = 512
    rope_dim: int = 64
    lru_w: int = 2048
    d_ff: int = 6144
    proj_pad: int = 512


DIMS = Dims()


def _params(n_grid):
    return pltpu.CompilerParams(dimension_semantics=("arbitrary",) * n_grid,
                                vmem_limit_bytes=VMEM_LIMIT)


def _tile(dim, pref, align):
    best = None
    t = align
    while t <= min(dim, pref):
        if dim % t == 0:
            best = t
        t += align
    return best if best is not None else dim


@dataclasses.dataclass(frozen=True)
class ProjLayout:
    q: int
    k: int
    v: int
    cq: int
    ckv: int
    krope: int
    lx: int
    lg: int
    gate: int
    total: int


def _proj_layout(dm, d_model):
    q = 0
    k = q + dm.gqa_heads * LANES
    v = k + dm.gqa_kv * LANES
    cq = v + dm.gqa_kv * LANES
    ckv = cq + dm.q_rank
    krope = ckv + dm.kv_rank
    attn_end = krope + LANES
    lx = -(-attn_end // dm.proj_pad) * dm.proj_pad
    lg = lx + dm.lru_w
    gate = lg + dm.lru_w
    total = gate + 3 * d_model
    return ProjLayout(q, k, v, cq, ckv, krope, lx, lg, gate, total)


def _mod_kernel(c_ref, w_ref, b_ref, o_ref):
    c = c_ref[...]
    a = (c * jax.nn.sigmoid(c)).astype(BF16)
    acc = jnp.dot(a, w_ref[...].astype(BF16), preferred_element_type=F32)
    o_ref[...] = acc + b_ref[...]


def _mod_vectors(c_rows, w_mod, b_mod):
    n_layers, d, n = w_mod.shape
    tn = _tile(n, 512, LANES)
    return pl.pallas_call(
        _mod_kernel,
        grid=(n_layers, n // tn),
        in_specs=[
            pl.BlockSpec((SUBLANES, d), lambda l, j: (0, 0)),
            pl.BlockSpec((None, d, tn), lambda l, j: (l, 0, j)),
            pl.BlockSpec((None, 1, tn), lambda l, j: (l, 0, j)),
        ],
        out_specs=pl.BlockSpec((None, SUBLANES, tn), lambda l, j: (l, 0, j)),
        out_shape=jax.ShapeDtypeStruct((n_layers, SUBLANES, n), F32),
        compiler_params=_params(2),
        name="mod_vectors",
    )(c_rows, w_mod, b_mod.reshape(n_layers, 1, n))


def _rms(x, g):
    return x * lax.rsqrt(jnp.mean(x * x, axis=-1, keepdims=True) + NORM_EPS) * g


def _adaln_kernel(xl_ref, xc_ref, g_ref, shift_ref, scale_ref, o_ref, *, n_lat):
    def emit(x_ref):
        y = _rms(x_ref[...], g_ref[...])
        o_ref[...] = (y * (1.0 + scale_ref[0]) + shift_ref[0]).astype(o_ref.dtype)

    t = pl.program_id(1)
    pl.when(t < n_lat)(lambda: emit(xl_ref))
    pl.when(t >= n_lat)(lambda: emit(xc_ref))


def _adaln_norm(x_lat, x_ctx, g, mods, idx_shift, idx_scale, with_ctx):
    b, tl, d = x_lat.shape
    tc = x_ctx.shape[1]
    n_lat = tl // ROW_TILE
    n_ctx = tc // ROW_TILE if with_ctx else 0
    n_rows = tl + (tc if with_ctx else 0)

    def mod_row(bi, t):
        return jnp.where(t < n_lat, bi, 2) * N_MOD

    return pl.pallas_call(
        functools.partial(_adaln_kernel, n_lat=n_lat),
        grid=(b, n_lat + n_ctx),
        in_specs=[
            pl.BlockSpec((None, ROW_TILE, d), lambda bi, t: (bi, jnp.minimum(t, n_lat - 1), 0)),
            pl.BlockSpec((None, ROW_TILE, d), lambda bi, t: (bi, jnp.maximum(t - n_lat, 0), 0)),
            pl.BlockSpec((1, d), lambda bi, t: (0, 0)),
            pl.BlockSpec((1, 1, d), lambda bi, t: (mod_row(bi, t) + idx_shift, 0, 0)),
            pl.BlockSpec((1, 1, d), lambda bi, t: (mod_row(bi, t) + idx_scale, 0, 0)),
        ],
        out_specs=pl.BlockSpec((None, ROW_TILE, d), lambda bi, t: (bi, t, 0)),
        out_shape=jax.ShapeDtypeStruct((b, n_rows, d), BF16),
        compiler_params=_params(2),
        name="adaln_norm",
    )(x_lat, x_ctx, g.reshape(1, d), mods, mods)


def _final_norm_kernel(x_ref, g_ref, o_ref):
    o_ref[...] = _rms(x_ref[...], g_ref[...])


def _final_norm(x, g):
    b, t, d = x.shape
    return pl.pallas_call(
        _final_norm_kernel,
        grid=(b, t // ROW_TILE),
        in_specs=[pl.BlockSpec((None, ROW_TILE, d), lambda bi, i: (bi, i, 0)),
                  pl.BlockSpec((1, d), lambda bi, i: (0, 0))],
        out_specs=pl.BlockSpec((None, ROW_TILE, d), lambda bi, i: (bi, i, 0)),
        out_shape=jax.ShapeDtypeStruct((b, t, d), F32),
        compiler_params=_params(2),
        name="final_norm",
    )(x, g.reshape(1, d))


def _mm_nt_kernel(a_ref, wt_ref, o_ref):
    o_ref[...] = lax.dot_general(a_ref[...], wt_ref[...], (((1,), (1,)), ((), ())),
                                 preferred_element_type=F32).astype(o_ref.dtype)


def _matmul_nt(a, wt, tm_pref=1536, tn_pref=512):
    m, k = a.shape
    n = wt.shape[0]
    tm = _tile(m, tm_pref, ROW_TILE)
    tn = _tile(n, tn_pref, LANES)
    return pl.pallas_call(
        _mm_nt_kernel,
        grid=(m // tm, n // tn),
        in_specs=[pl.BlockSpec((tm, k), lambda i, j: (i, 0)),
                  pl.BlockSpec((tn, k), lambda i, j: (j, 0))],
        out_specs=pl.BlockSpec((tm, tn), lambda i, j: (i, j)),
        out_shape=jax.ShapeDtypeStruct((m, n), BF16),
        compiler_params=_params(2),
        name="matmul_nt",
    )(a, wt)


def _swiglu_kernel(a_ref, wg_ref, wu_ref, o_ref):
    a = a_ref[...]
    g = jnp.dot(a, wg_ref[...], preferred_element_type=F32)
    u = jnp.dot(a, wu_ref[...], preferred_element_type=F32)
    o_ref[...] = (g * jax.nn.sigmoid(g) * u).astype(o_ref.dtype)


def _swiglu_in(h, w_in, tm_pref=1536, tn_pref=512):
    m, k = h.shape
    f = w_in.shape[1] // 2
    tm = _tile(m, tm_pref, ROW_TILE)
    tn = _tile(f, tn_pref, LANES)
    nj = f // tn
    return pl.pallas_call(
        _swiglu_kernel,
        grid=(m // tm, nj),
        in_specs=[pl.BlockSpec((tm, k), lambda i, j: (i, 0)),
                  pl.BlockSpec((k, tn), lambda i, j: (0, j)),
                  pl.BlockSpec((k, tn), lambda i, j: (0, j + nj))],
        out_specs=pl.BlockSpec((tm, tn), lambda i, j: (i, j)),
        out_shape=jax.ShapeDtypeStruct((m, f), BF16),
        compiler_params=_params(2),
        name="swiglu_in",
    )(h, w_in, w_in)


def _residual_kernel(a_ref, w_ref, x_ref, gate_ref, o_ref, *, coef):
    y = jnp.dot(a_ref[...], w_ref[...], preferred_element_type=F32)
    gate = gate_ref[0] if coef is None else coef * gate_ref[0]
    o_ref[...] = x_ref[...] + gate * y


def _residual_matmul(a, w, x, mods, idx_gate, coef, a_row_off, is_ctx, tm_pref=1024, tn_pref=512):
    b, r, n = x.shape
    k = a.shape[2]
    tm = _tile(r, tm_pref, ROW_TILE)
    tn = _tile(n, tn_pref, LANES)
    off = a_row_off // tm

    def gate_row(bi):
        return (2 if is_ctx else bi) * N_MOD + idx_gate

    return pl.pallas_call(
        functools.partial(_residual_kernel, coef=coef),
        grid=(b, r // tm, n // tn),
        in_specs=[pl.BlockSpec((None, tm, k), lambda bi, i, j: (bi, i + off, 0)),
                  pl.BlockSpec((k, tn), lambda bi, i, j: (0, j)),
                  pl.BlockSpec((None, tm, tn), lambda bi, i, j: (bi, i, j)),
                  pl.BlockSpec((1, 1, tn), lambda bi, i, j: (gate_row(bi), 0, j))],
        out_specs=pl.BlockSpec((None, tm, tn), lambda bi, i, j: (bi, i, j)),
        out_shape=jax.ShapeDtypeStruct((b, r, n), F32),
        compiler_params=_params(3),
        name="residual_matmul",
    )(a, w, x, mods)


def _merge_kernel(a0_ref, a1_ref, a2_ref, w0_ref, w1_ref, w2_ref, g0_ref, g1_ref, g2_ref, o_ref):
    def branch(a_ref, w_ref, g_ref):
        y = jnp.dot(a_ref[...], w_ref[...], preferred_element_type=F32)
        return jax.nn.sigmoid(g_ref[...].astype(F32)) * y

    acc = branch(a0_ref, w0_ref, g0_ref) + branch(a1_ref, w1_ref, g1_ref)
    o_ref[...] = (acc + branch(a2_ref, w2_ref, g2_ref)).astype(o_ref.dtype)


def _merge(o_gqa, o_mla, y_lru, proj, w0, w1, w2, gate_col, rows, mix_row_off, tm_pref=1024, tn_pref=512):
    b = o_gqa.shape[0]
    n = w0.shape[1]
    tm = _tile(rows, tm_pref, ROW_TILE)
    tn = _tile(n, tn_pref, LANES)
    off = mix_row_off // tm
    gcol = gate_col // tn
    nj = n // tn

    def a_spec(a, row_off):
        return pl.BlockSpec((None, tm, a.shape[2]), lambda bi, i, j: (bi, i + row_off, 0))

    def w_spec(w):
        return pl.BlockSpec((w.shape[0], tn), lambda bi, i, j: (0, j))

    def gate_spec(br):
        return pl.BlockSpec((None, tm, tn), lambda bi, i, j: (bi, i + off, gcol + br * nj + j))

    return pl.pallas_call(
        _merge_kernel,
        grid=(b, rows // tm, nj),
        in_specs=[a_spec(o_gqa, 0), a_spec(o_mla, 0), a_spec(y_lru, off), w_spec(w0), w_spec(w1), w_spec(w2),
                  gate_spec(0), gate_spec(1), gate_spec(2)],
        out_specs=pl.BlockSpec((None, tm, tn), lambda bi, i, j: (bi, i, j)),
        out_shape=jax.ShapeDtypeStruct((b, rows, n), BF16),
        compiler_params=_params(3),
        name="merge",
    )(o_gqa, o_mla, y_lru, w0, w1, w2, proj, proj, proj)


def _rope_tables(n_tok, rot_dim, grid_w, ctx_len):
    rows = n_tok // grid_w
    r_idx, c_idx = jnp.meshgrid(jnp.arange(rows), jnp.arange(grid_w), indexing='ij')
    r_idx = r_idx.reshape(-1).astype(F32)
    c_idx = c_idx.reshape(-1).astype(F32)
    n_pairs = rot_dim // 4
    freqs = jnp.power(ROPE_THETA, -jnp.arange(n_pairs, dtype=F32) / n_pairs)
    ang = jnp.concatenate([r_idx[:, None] * freqs, c_idx[:, None] * freqs], axis=-1)
    cos, sin = jnp.cos(ang), jnp.sin(ang)
    cos_full = jnp.concatenate([cos, cos], axis=-1)
    sin_signed = jnp.concatenate([-sin, sin], axis=-1)
    reps = LANES // rot_dim
    cos_full = jnp.tile(cos_full, (1, reps))
    sin_signed = jnp.tile(sin_signed, (1, reps))
    cos_full = jnp.concatenate([cos_full, jnp.ones((ctx_len, LANES), F32)], axis=0)
    sin_signed = jnp.concatenate([sin_signed, jnp.zeros((ctx_len, LANES), F32)], axis=0)
    return cos_full, sin_signed


def _rope128(x, cos, sin):
    return x * cos + pltpu.roll(x, LANES // 2, 1) * sin


def _rope64(x, cos, sin):
    lane = lax.broadcasted_iota(jnp.int32, x.shape, 1)
    first_half = (lane % 64) < 32
    partner = jnp.where(first_half, pltpu.roll(x, LANES - 32, 1), pltpu.roll(x, 32, 1))
    return x * cos + partner * sin


def _prep_kernel(k_ref, cq_ref, ckv_ref, kr_ref, cg_ref, sg_ref, cm_ref, sm_ref,
                 gk_ref, gq_ref, gkv_ref, wuq_ref, wukv_ref, ko_ref, qup_ref, kvup_ref, kro_ref,
                 *, kv_heads):
    cos_g, sin_g = cg_ref[...], sg_ref[...]
    for h in range(kv_heads):
        cols = slice(h * LANES, (h + 1) * LANES)
        y = _rms(k_ref[:, cols].astype(F32), gk_ref[...])
        ko_ref[:, cols] = _rope128(y, cos_g, sin_g).astype(BF16)
    cq_n = _rms(cq_ref[...].astype(F32), gq_ref[...]).astype(BF16)
    qup_ref[...] = jnp.dot(cq_n, wuq_ref[...], preferred_element_type=F32).astype(BF16)
    ckv_n = _rms(ckv_ref[...].astype(F32), gkv_ref[...]).astype(BF16)
    kvup_ref[...] = jnp.dot(ckv_n, wukv_ref[...], preferred_element_type=F32).astype(BF16)
    kro_ref[...] = _rope64(kr_ref[...].astype(F32), cm_ref[...], sm_ref[...]).astype(BF16)


def _prep(proj, lay, dm, tabs_g, tabs_m, gk, gq, gkv, w_uq, w_ukv):
    b, tt, _ = proj.shape
    n_up = w_uq.shape[1]
    kw = dm.gqa_kv * LANES
    assert lay.k % kw == 0 and lay.cq % dm.q_rank == 0 and lay.ckv % dm.kv_rank == 0
    row = lambda bi, t: (bi, t, 0)
    tab = pl.BlockSpec((ROW_TILE, LANES), lambda bi, t: (t, 0))

    def vec(n):
        return pl.BlockSpec((1, n), lambda bi, t: (0, 0))

    return pl.pallas_call(
        functools.partial(_prep_kernel, kv_heads=dm.gqa_kv),
        grid=(b, tt // ROW_TILE),
        in_specs=[pl.BlockSpec((None, ROW_TILE, kw), lambda bi, t: (bi, t, lay.k // kw)),
                  pl.BlockSpec((None, ROW_TILE, dm.q_rank), lambda bi, t: (bi, t, lay.cq // dm.q_rank)),
                  pl.BlockSpec((None, ROW_TILE, dm.kv_rank), lambda bi, t: (bi, t, lay.ckv // dm.kv_rank)),
                  pl.BlockSpec((None, ROW_TILE, LANES), lambda bi, t: (bi, t, lay.krope // LANES)),
                  tab, tab, tab, tab, vec(LANES), vec(dm.q_rank), vec(dm.kv_rank),
                  pl.BlockSpec((dm.q_rank, n_up), lambda bi, t: (0, 0)),
                  pl.BlockSpec((dm.kv_rank, n_up), lambda bi, t: (0, 0))],
        out_specs=[pl.BlockSpec((None, ROW_TILE, kw), row),
                   pl.BlockSpec((None, ROW_TILE, n_up), row),
                   pl.BlockSpec((None, ROW_TILE, n_up), row),
                   pl.BlockSpec((None, ROW_TILE, LANES), row)],
        out_shape=[jax.ShapeDtypeStruct((b, tt, kw), BF16),
                   jax.ShapeDtypeStruct((b, tt, n_up), BF16),
                   jax.ShapeDtypeStruct((b, tt, n_up), BF16),
                   jax.ShapeDtypeStruct((b, tt, LANES), BF16)],
        compiler_params=_params(2),
        name="prep",
    )(proj, proj, proj, proj, tabs_g[0], tabs_g[1], tabs_m[0], tabs_m[1],
      gk.reshape(1, LANES), gq.reshape(1, dm.q_rank), gkv.reshape(1, dm.kv_rank), w_uq, w_ukv)


LOG2E = 1.4426950408889634
KEY_CHUNK = 768


SCORE_BUFFERS = 4
ONES_ROWS = 16


def _flash_scratch(dk, n_q, key_rows, tk):
    return ([pltpu.VMEM((dk, n_q), BF16),
             pltpu.VMEM((key_rows // tk, LANES + ONES_ROWS, tk), BF16),
             pltpu.VMEM((1, n_q), F32),
             pltpu.VMEM((LANES + ONES_ROWS, n_q), F32)]
            + [pltpu.VMEM((tk, n_q), F32)] * SCORE_BUFFERS
            + [pltpu.VMEM((1, n_q), F32)] * SCORE_BUFFERS)


def _flash_t(qt_ref, load_k, load_vt, n_chunks, stats):
    m_ref, acc_ref = stats[:2]
    s_refs = stats[2:2 + SCORE_BUFFERS]
    smax_refs = stats[2 + SCORE_BUFFERS:]
    m_ref[...] = jnp.full(m_ref.shape, -jnp.inf, F32)
    acc_ref[...] = jnp.zeros(acc_ref.shape, F32)

    def scores(c, slot):
        s = jnp.dot(load_k(c), qt_ref[...], preferred_element_type=F32)
        s_refs[slot][...] = s
        smax_refs[slot][...] = jnp.max(s, axis=0, keepdims=True)

    def update(c, slot):
        m_prev = m_ref[...]
        m_new = jnp.maximum(m_prev, smax_refs[slot][...])
        p = jnp.exp2(s_refs[slot][...] - m_new)
        alpha = jnp.exp2(m_prev - m_new)
        acc_ref[...] = alpha * acc_ref[...] + jnp.dot(load_vt(c), p.astype(BF16),
                                                     preferred_element_type=F32)
        m_ref[...] = m_new

    scores(0, 0)
    n_main = (n_chunks - 1) // SCORE_BUFFERS

    def body(i, carry):
        c = SCORE_BUFFERS * i
        for j in range(SCORE_BUFFERS):
            scores(c + j + 1, (j + 1) % SCORE_BUFFERS)
            update(c + j, j)
        return carry

    lax.fori_loop(0, n_main, body, 0)
    for c in range(SCORE_BUFFERS * n_main, n_chunks):
        if c + 1 < n_chunks:
            scores(c + 1, (c + 1) % SCORE_BUFFERS)
        update(c, c % SCORE_BUFFERS)
    return acc_ref[:LANES, :] / acc_ref[LANES:LANES + 1, :]


def _stage_values_t(v_ref, vt_ref, tk):
    @pl.when(pl.program_id(2) == 0)
    def _():
        for c in range(vt_ref.shape[0]):
            vt_ref[c, :LANES, :] = v_ref[c * tk:(c + 1) * tk, :].astype(F32).T.astype(BF16)
            vt_ref[c, LANES:, :] = jnp.ones((ONES_ROWS, tk), BF16)


def _gqa_kernel(q_ref, k_ref, v_ref, cos_ref, sin_ref, gq_ref, o_ref, qt_ref, vt_ref, *stats,
                group, tq, tk, scale):
    _stage_values_t(v_ref, vt_ref, tk)
    cos, sin = cos_ref[...], sin_ref[...]
    for g in range(group):
        y = _rms(q_ref[:, g * LANES:(g + 1) * LANES].astype(F32), gq_ref[...])
        qt_ref[:, g * tq:(g + 1) * tq] = (_rope128(y, cos, sin) * scale).T.astype(BF16)
    out_t = _flash_t(qt_ref,
                     lambda c: k_ref[pl.ds(pl.multiple_of(c * tk, tk), tk), :],
                     lambda c: vt_ref[c],
                     vt_ref.shape[0], stats)
    for g in range(group):
        o_ref[:, g * LANES:(g + 1) * LANES] = out_t[:, g * tq:(g + 1) * tq].T.astype(o_ref.dtype)


def _gqa_attention(proj, k_prep, lay, dm, tabs, gq, q_rows, q_off, key_rows, key_off, tq_pref=512):
    b = proj.shape[0]
    group = dm.gqa_heads // dm.gqa_kv
    gw = group * LANES
    tq = _tile(q_rows, tq_pref, ROW_TILE)
    tk = _tile(key_rows, KEY_CHUNK, ROW_TILE)
    assert q_off % tq == 0 and key_off % key_rows == 0
    qo, ko, vcol = q_off // tq, key_off // key_rows, lay.v // LANES
    return pl.pallas_call(
        functools.partial(_gqa_kernel, group=group, tq=tq, tk=tk, scale=float(LANES) ** -0.5 * LOG2E),
        grid=(b, dm.gqa_kv, q_rows // tq),
        in_specs=[pl.BlockSpec((None, tq, gw), lambda bi, h, i: (bi, i + qo, h)),
                  pl.BlockSpec((None, key_rows, LANES), lambda bi, h, i: (bi, ko, h)),
                  pl.BlockSpec((None, key_rows, LANES), lambda bi, h, i: (bi, ko, vcol + h)),
                  pl.BlockSpec((tq, LANES), lambda bi, h, i: (i + qo, 0)),
                  pl.BlockSpec((tq, LANES), lambda bi, h, i: (i + qo, 0)),
                  pl.BlockSpec((1, LANES), lambda bi, h, i: (0, 0))],
        out_specs=pl.BlockSpec((None, tq, gw), lambda bi, h, i: (bi, i, h)),
        out_shape=jax.ShapeDtypeStruct((b, q_rows, dm.gqa_heads * LANES), BF16),
        scratch_shapes=_flash_scratch(LANES, group * tq, key_rows, tk),
        compiler_params=_params(3),
        name="gqa_attention",
    )(proj, k_prep, proj, tabs[0], tabs[1], gq.reshape(1, LANES))


def _mla_kernel(qn_ref, qr_ref, kn_ref, kr_ref, v_ref, cos_ref, sin_ref, o_ref,
                qt_ref, vt_ref, *stats, tk, scale):
    _stage_values_t(v_ref, vt_ref, tk)
    qt_ref[:LANES, :] = (qn_ref[...].astype(F32) * scale).T.astype(BF16)
    qr = _rope64(qr_ref[...].astype(F32), cos_ref[...], sin_ref[...])
    qt_ref[LANES:, :] = (qr * scale).T.astype(BF16)

    def load_k(c):
        rows = pl.ds(pl.multiple_of(c * tk, tk), tk)
        return jnp.concatenate([kn_ref[rows, :], kr_ref[rows, :]], axis=1)

    out_t = _flash_t(qt_ref, load_k, lambda c: vt_ref[c], vt_ref.shape[0], stats)
    o_ref[...] = out_t.T.astype(o_ref.dtype)


def _mla_attention(q_up, kv_up, k_rope, dm, tabs, q_rows, q_off, key_rows, key_off, tq_pref=2048):
    b = q_up.shape[0]
    h_n = dm.mla_heads
    tq = _tile(q_rows, tq_pref, ROW_TILE)
    tk = _tile(key_rows, KEY_CHUNK, ROW_TILE)
    assert q_off % tq == 0 and key_off % key_rows == 0
    qo, ko = q_off // tq, key_off // key_rows
    scale = float(LANES + dm.rope_dim) ** -0.5 * LOG2E
    return pl.pallas_call(
        functools.partial(_mla_kernel, tk=tk, scale=scale),
        grid=(b, h_n, q_rows // tq),
        in_specs=[pl.BlockSpec((None, tq, LANES), lambda bi, h, i: (bi, i + qo, h)),
                  pl.BlockSpec((None, tq, LANES), lambda bi, h, i: (bi, i + qo, h_n + h)),
                  pl.BlockSpec((None, key_rows, LANES), lambda bi, h, i: (bi, ko, h)),
                  pl.BlockSpec((None, key_rows, LANES), lambda bi, h, i: (bi, ko, 0)),
                  pl.BlockSpec((None, key_rows, LANES), lambda bi, h, i: (bi, ko, h_n + h)),
                  pl.BlockSpec((tq, LANES), lambda bi, h, i: (i + qo, 0)),
                  pl.BlockSpec((tq, LANES), lambda bi, h, i: (i + qo, 0))],
        out_specs=pl.BlockSpec((None, tq, LANES), lambda bi, h, i: (bi, i, h)),
        out_shape=jax.ShapeDtypeStruct((b, q_rows, h_n * LANES), BF16),
        scratch_shapes=_flash_scratch(2 * LANES, tq, key_rows, tk),
        compiler_params=_params(3),
        name="mla_attention",
    )(q_up, q_up, kv_up, k_rope, kv_up, tabs[0], tabs[1])


def _lru_kernel(lx_ref, lg_ref, cw_ref, cb_ref, wa_ref, ba_ref, wi_ref, bi_ref, lam_ref, o_ref,
                xs_ref, xc_ref, hf_ref, hb_ref, *, tl, tc, ch):
    tt = tl + tc
    n_lat, n_ctx = tl // ch, tc // ch
    pad = SUBLANES
    zeros = jnp.zeros((pad, LANES), F32)
    xs_ref[0:pad, :] = zeros
    xs_ref[pad + tt:2 * pad + tt, :] = zeros

    def stage(c, carry):
        r = pl.multiple_of(c * ch, ch)
        xs_ref[pl.ds(pad + r, ch), :] = lx_ref[pl.ds(r, ch), :].astype(F32)
        return carry

    lax.fori_loop(0, n_lat + n_ctx, stage, 0)

    def conv(c, carry):
        r = pl.multiple_of(c * ch, ch)
        ext = xs_ref[pl.ds(r, ch + 2 * pad), :]
        n_ext = ch + 2 * pad
        starts = jnp.logical_or(c == 0, c == n_lat)
        ends = jnp.logical_or(c == n_lat - 1, c == n_lat + n_ctx - 1)
        head = jnp.where(starts, 0.0, ext[:pad])
        tail = jnp.where(ends, 0.0, ext[pad + ch:])
        ext = jnp.concatenate([head, ext[pad:pad + ch], tail], axis=0)
        y = cb_ref[...]
        for j in range(CONV_WIDTH):
            shift = j - 1
            tap = ext if shift == 0 else pltpu.roll(ext, (-shift) % n_ext, 0)
            y = y + tap[pad:pad + ch] * cw_ref[j:j + 1, :]
        xc_ref[pl.ds(r, ch), :] = y
        return carry

    lax.fori_loop(0, n_lat + n_ctx, conv, 0)

    row = lax.broadcasted_iota(jnp.int32, (SUBLANES, LANES), 0)

    def coeffs(xc, d):
        xb = xc.astype(BF16)
        r = jax.nn.sigmoid(jnp.dot(xb, wa_ref[d].astype(BF16), preferred_element_type=F32) + ba_ref[d:d + 1, :])
        i = jax.nn.sigmoid(jnp.dot(xb, wi_ref[d].astype(BF16), preferred_element_type=F32) + bi_ref[d:d + 1, :])
        z = -lam_ref[d:d + 1, :]
        softplus = jnp.maximum(z, 0.0) + jnp.log1p(jnp.exp(-jnp.abs(z)))
        log_a = -LRU_C * r * softplus
        a = jnp.exp(log_a)
        u = jnp.sqrt(1.0 - a * a) * (i * xc)
        return a, u

    def scan_chunk(c, h, d, out_ref):
        r = pl.multiple_of(c * ch, ch)
        a, u = coeffs(xc_ref[pl.ds(r, ch), :], d)
        n_blk = ch // SUBLANES
        order = range(n_blk) if d == 0 else range(n_blk - 1, -1, -1)
        edge = SUBLANES - 1 if d == 0 else 0
        blocks = []
        for j in range(n_blk):
            aj = a[j * SUBLANES:(j + 1) * SUBLANES]
            uj = u[j * SUBLANES:(j + 1) * SUBLANES]
            for dist in (1, 2, 4):
                if d == 0:
                    shift, mask = dist, row >= dist
                else:
                    shift, mask = SUBLANES - dist, row < SUBLANES - dist
                a_s = jnp.where(mask, pltpu.roll(aj, shift, 0), 1.0)
                u_s = jnp.where(mask, pltpu.roll(uj, shift, 0), 0.0)
                uj = uj + aj * u_s
                aj = aj * a_s
            blocks.append((aj, uj))
        carry_in = [None] * n_blk
        for j in order:
            carry_in[j] = h
            aj, uj = blocks[j]
            h = aj[edge:edge + 1] * h + uj[edge:edge + 1]
        states = [blocks[j][0] * carry_in[j] + blocks[j][1] for j in range(n_blk)]
        out_ref[pl.ds(r, ch), :] = jnp.concatenate(states, axis=0)
        return h

    def both(fwd_chunk, bwd_chunk, carry):
        return (scan_chunk(fwd_chunk, carry[0], 0, hf_ref), scan_chunk(bwd_chunk, carry[1], 1, hb_ref))

    h0 = jnp.zeros((1, LANES), F32)
    carry = lax.fori_loop(0, n_ctx, lambda k, cr: both(n_lat + k, n_lat + n_ctx - 1 - k, cr), (h0, h0))
    lax.fori_loop(0, n_lat, lambda k, cr: both(k, n_lat - 1 - k, cr), carry)

    def combine(c, carry):
        rows = pl.ds(pl.multiple_of(c * ch, ch), ch)
        gate = jax.nn.gelu(lg_ref[rows, :].astype(F32))
        o_ref[rows, :] = ((hf_ref[rows, :] + hb_ref[rows, :]) * gate).astype(o_ref.dtype)
        return carry

    lax.fori_loop(0, n_lat + n_ctx, combine, 0)


def _lru(proj, lay, dm, tl, conv_w, conv_b, wa, ba, wi, bi, lam):
    b, tt, _ = proj.shape
    nb = dm.lru_w // LANES
    xcol, gcol = lay.lx // LANES, lay.lg // LANES
    vec2 = pl.BlockSpec((2, LANES), lambda bi_, n: (0, n))
    wspec = pl.BlockSpec((2, None, LANES, LANES), lambda bi_, n: (0, n, 0, 0))
    return pl.pallas_call(
        functools.partial(_lru_kernel, tl=tl, tc=tt - tl, ch=ROW_TILE),
        grid=(b, nb),
        in_specs=[pl.BlockSpec((None, tt, LANES), lambda bi_, n: (bi_, 0, xcol + n)),
                  pl.BlockSpec((None, tt, LANES), lambda bi_, n: (bi_, 0, gcol + n)),
                  pl.BlockSpec((CONV_WIDTH, LANES), lambda bi_, n: (0, n)),
                  pl.BlockSpec((1, LANES), lambda bi_, n: (0, n)),
                  wspec, vec2, wspec, vec2, vec2],
        out_specs=pl.BlockSpec((None, tt, LANES), lambda bi_, n: (bi_, 0, n)),
        out_shape=jax.ShapeDtypeStruct((b, tt, dm.lru_w), BF16),
        scratch_shapes=[pltpu.VMEM((tt + 2 * SUBLANES, LANES), F32),
                        pltpu.VMEM((tt, LANES), F32),
                        pltpu.VMEM((tt, LANES), F32),
                        pltpu.VMEM((tt, LANES), F32)],
        compiler_params=_params(2),
        name="rglru",
    )(proj, proj, conv_w, conv_b.reshape(1, dm.lru_w), wa, ba, wi, bi, lam)


def _cast_kernel(w_ref, o_ref):
    o_ref[...] = w_ref[...].astype(o_ref.dtype)


def _cast_layer(w, l):
    _, r, c = w.shape
    tr = _tile(r, 512, SUBLANES)
    tc = _tile(c, 2048, LANES)
    return pl.pallas_call(
        _cast_kernel,
        grid=(r // tr, c // tc),
        in_specs=[pl.BlockSpec((None, tr, tc), lambda i, j: (l, i, j))],
        out_specs=pl.BlockSpec((tr, tc), lambda i, j: (i, j)),
        out_shape=jax.ShapeDtypeStruct((r, c), BF16),
        compiler_params=_params(2),
        name="cast_weight",
    )(w)


def _proj_weight_kernel(a_ref, b_ref, o_ref, *, n_copy, n_lx, shift):
    j = pl.program_id(0)

    @pl.when(j < n_copy)
    def _():
        o_ref[...] = b_ref[...].astype(o_ref.dtype)

    @pl.when(j == n_copy)
    def _():
        row = lax.broadcasted_iota(jnp.int32, b_ref.shape, 0)
        o_ref[...] = jnp.where(row < shift, b_ref[...], 0.0).astype(o_ref.dtype)

    @pl.when(jnp.logical_and(j > n_copy, j < n_lx))
    def _():
        o_ref[...] = jnp.zeros(o_ref.shape, o_ref.dtype)

    @pl.when(j >= n_lx)
    def _():
        o_ref[...] = jnp.concatenate([a_ref[shift:, :], b_ref[:shift, :]], axis=0).astype(o_ref.dtype)


def _proj_weight_t(w_in_t, l, lay, dm):
    _, n_in, d = w_in_t.shape
    real_attn = lay.krope + dm.rope_dim
    bw = 512
    while lay.krope % bw or lay.lx % bw:
        bw //= 2
    shift = real_attn - lay.krope
    assert bw >= SUBLANES and shift < bw and shift % SUBLANES == 0
    n_copy = lay.krope // bw
    n_lx = lay.lx // bw
    last_in = (n_in - 1) // bw

    def a_idx(j):
        return jnp.clip(n_copy + j - n_lx, 0, last_in)

    def b_idx(j):
        return jnp.minimum(jnp.where(j < n_lx, j, n_copy + j - n_lx + 1), last_in)

    return pl.pallas_call(
        functools.partial(_proj_weight_kernel, n_copy=n_copy, n_lx=n_lx, shift=shift),
        grid=(lay.total // bw,),
        in_specs=[pl.BlockSpec((None, bw, d), lambda j: (l, a_idx(j), 0)),
                  pl.BlockSpec((None, bw, d), lambda j: (l, b_idx(j), 0))],
        out_specs=pl.BlockSpec((bw, d), lambda j: (j, 0)),
        out_shape=jax.ShapeDtypeStruct((lay.total, d), BF16),
        compiler_params=_params(1),
        name="proj_weight",
    )(w_in_t, w_in_t)


def _layer_weights(l, dm, lay, w_ff1_in, w_ff1_out, w_ff2_in, w_ff2_out, w_in, w_mla_uq, w_mla_ukv,
                   w_o_gqa, w_o_mla, w_o_lru, w_out):
    w_proj = _proj_weight_t(jnp.swapaxes(w_in, 1, 2), l, lay, dm)
    h = dm.mla_heads
    uq = w_mla_uq[l].reshape(dm.q_rank, h, LANES + dm.rope_dim)
    uq_rope = jnp.pad(uq[:, :, LANES:], ((0, 0), (0, 0), (0, LANES - dm.rope_dim)))
    w_uq = jnp.concatenate([uq[:, :, :LANES].reshape(dm.q_rank, h * LANES),
                            uq_rope.reshape(dm.q_rank, h * LANES)], axis=1).astype(BF16)
    ukv = w_mla_ukv[l].reshape(dm.kv_rank, h, 2 * LANES)
    w_ukv = jnp.concatenate([ukv[:, :, :LANES].reshape(dm.kv_rank, h * LANES),
                             ukv[:, :, LANES:].reshape(dm.kv_rank, h * LANES)], axis=1).astype(BF16)
    cast = functools.partial(_cast_layer, l=l)
    return dict(ff1_in=cast(w_ff1_in), ff1_out=cast(w_ff1_out), ff2_in=cast(w_ff2_in), ff2_out=cast(w_ff2_out),
                proj=w_proj, uq=w_uq, ukv=w_ukv, o_gqa=cast(w_o_gqa), o_mla=cast(w_o_mla),
                o_lru=cast(w_o_lru), out=cast(w_out))


def _forward(dm, x, c, ctx, c_ctx, w_mod, b_mod, g_norm, w_ff1_in, w_ff1_out, w_ff2_in, w_ff2_out, w_in,
             gqa_q_norm, gqa_k_norm, mla_q_norm, mla_kv_norm, w_mla_uq, w_mla_ukv, lru_conv_w, lru_conv_b,
             lru_wa, lru_ba, lru_wi, lru_bi, lru_lambda, w_o_gqa, w_o_mla, w_o_lru, w_out, final_norm):
    b, tl, d = x.shape
    tc = ctx.shape[1]
    tt = tl + tc
    depth = w_mod.shape[0]
    assert b == 2 and tl % ROW_TILE == 0 and tc % ROW_TILE == 0
    lay = _proj_layout(dm, d)
    tabs_g = _rope_tables(tl, LANES, dm.grid_w, tc)
    tabs_m = _rope_tables(tl, dm.rope_dim, dm.grid_w, tc)

    c_rows = jnp.concatenate([c, c_ctx[None, :], jnp.zeros((SUBLANES - b - 1, d), F32)], axis=0)
    mods_all = _mod_vectors(c_rows, w_mod, b_mod)

    xc = ctx
    for l in range(depth):
        last = l == depth - 1
        w = _layer_weights(l, dm, lay, w_ff1_in, w_ff1_out, w_ff2_in, w_ff2_out, w_in, w_mla_uq, w_mla_ukv,
                           w_o_gqa, w_o_mla, w_o_lru, w_out)
        mods = mods_all[l, :b + 1].reshape((b + 1) * N_MOD, 1, d)

        h1 = _adaln_norm(x, xc, g_norm[l, 0], mods, 0, 1, True)
        act = _swiglu_in(h1.reshape(b * tt, d), w['ff1_in']).reshape(b, tt, dm.d_ff)
        x = _residual_matmul(act, w['ff1_out'], x, mods, 2, 0.5, 0, False)
        xc = _residual_matmul(act, w['ff1_out'], xc, mods, 2, 0.5, tl, True)

        h2 = _adaln_norm(x, xc, g_norm[l, 1], mods, 3, 4, True)
        proj = _matmul_nt(h2.reshape(b * tt, d), w['proj']).reshape(b, tt, lay.total)
        k_gqa, q_up, kv_up, k_rope = _prep(proj, lay, dm, tabs_g, tabs_m, gqa_k_norm[l], mla_q_norm[l],
                                           mla_kv_norm[l], w['uq'], w['ukv'])
        y_lru = _lru(proj, lay, dm, tl, lru_conv_w[l], lru_conv_b[l], lru_wa[l], lru_ba[l], lru_wi[l],
                     lru_bi[l], lru_lambda[l])

        o_gqa = _gqa_attention(proj, k_gqa, lay, dm, tabs_g, gqa_q_norm[l], tl, 0, tt, 0)
        o_mla = _mla_attention(q_up, kv_up, k_rope, dm, tabs_m, tl, 0, tt, 0)
        merged = _merge(o_gqa, o_mla, y_lru, proj, w['o_gqa'], w['o_mla'], w['o_lru'], lay.gate, tl, 0)
        x = _residual_matmul(merged, w['out'], x, mods, 5, None, 0, False)
        if not last:
            o_gqa_c = _gqa_attention(proj, k_gqa, lay, dm, tabs_g, gqa_q_norm[l], tc, tl, tc, tl)
            o_mla_c = _mla_attention(q_up, kv_up, k_rope, dm, tabs_m, tc, tl, tc, tl)
            merged_c = _merge(o_gqa_c, o_mla_c, y_lru, proj, w['o_gqa'], w['o_mla'], w['o_lru'],
                              lay.gate, tc, tl)
            xc = _residual_matmul(merged_c, w['out'], xc, mods, 5, None, 0, True)

        h3 = _adaln_norm(x, xc, g_norm[l, 2], mods, 6, 7, not last)
        rows = h3.shape[1]
        act = _swiglu_in(h3.reshape(b * rows, d), w['ff2_in']).reshape(b, rows, dm.d_ff)
        x = _residual_matmul(act, w['ff2_out'], x, mods, 8, 0.5, 0, False)
        if not last:
            xc = _residual_matmul(act, w['ff2_out'], xc, mods, 8, 0.5, tl, True)
    return _final_norm(x, final_norm)


def kernel(x, c, ctx, c_ctx, w_mod, b_mod, g_norm, w_ff1_in, w_ff1_out, w_ff2_in, w_ff2_out, w_in, gqa_q_norm, gqa_k_norm, mla_q_norm, mla_kv_norm, w_mla_uq, w_mla_ukv, lru_conv_w, lru_conv_b, lru_wa, lru_ba, lru_wi, lru_bi, lru_lambda, w_o_gqa, w_o_mla, w_o_lru, w_out, final_norm):
    return _forward(DIMS, x, c, ctx, c_ctx, w_mod, b_mod, g_norm, w_ff1_in, w_ff1_out, w_ff2_in, w_ff2_out,
                    w_in, gqa_q_norm, gqa_k_norm, mla_q_norm, mla_kv_norm, w_mla_uq, w_mla_ukv, lru_conv_w,
                    lru_conv_b, lru_wa, lru_ba, lru_wi, lru_bi, lru_lambda, w_o_gqa, w_o_mla, w_o_lru, w_out,
                    final_norm)
```
